```python
import math
import jax
import jax.numpy as jnp
from jax import lax
import numpy as np

D_MODEL = 1024
BATCH = 4
SEQ = 4096
DEPTH = 4
DEC_BATCH = 128
DEC_SEQ = 4
PAST_LEN = 2048
PAGE_SIZE = 128

H_A = 8
D_HEAD_A = 64
W_A = H_A * D_HEAD_A
Q_BLOCK = 128
H_B = 8
D_HEAD_B = 64
W_B = H_B * D_HEAD_B
LORA_W = 64
LORA_A = 64
LORA_G = 128
R_B = 3 * W_B + LORA_W + LORA_A + LORA_G
LN_X_EPS = D_HEAD_B * 1e-5
H_C = 4
D_HEAD_C = 128
W_C = H_C * D_HEAD_C
CONV_W = 4
GDN_CHUNK = 64
D_FF = 2816
PROJ_SIZES = (3 * W_A, R_B, 3 * W_C, 2 * H_C, W_C, 3 * D_MODEL)
PROJ_W = sum(PROJ_SIZES)
NORM_EPS = 1e-6

kernel_name = 'hybrid_stickbreak_rwkv7_gdn_macaron_step'


def _rmsnorm(x, w):
    xf = x.astype(jnp.float32)
    y = xf * lax.rsqrt(jnp.mean(xf * xf, axis=-1, keepdims=True) + NORM_EPS)
    return (y * w.astype(jnp.float32)).astype(x.dtype)


def _l2norm(x):
    xf = x.astype(jnp.float32)
    return xf * lax.rsqrt(jnp.sum(xf * xf, axis=-1, keepdims=True) + 1e-6)


def _swiglu(x, w_gate, w_up, w_down):
    return (jax.nn.silu(x @ w_gate) * (x @ w_up)) @ w_down


def _split(x, sizes):
    idx = np.cumsum(sizes)[:-1].tolist()
    return jnp.split(x, idx, axis=-1)


def _sb_weights(z, q_pos, k_pos):
    causal = k_pos[None, :] < q_pos[:, None]
    log_keep = jnp.where(causal, jax.nn.log_sigmoid(-z), 0.0)
    log_rest = lax.cumsum(log_keep, axis=z.ndim - 1, reverse=True) - log_keep
    return jnp.where(causal, jnp.exp(jax.nn.log_sigmoid(z) + log_rest), 0.0)


def _sb_prompt(q, k, v, bias):
    b, t, h, d = q.shape
    qf, kf, vf = (x.astype(jnp.float32) for x in (q, k, v))
    bias = bias.astype(jnp.float32)[None, :, None, None]
    k_pos = jnp.arange(t)

    def block(i):
        start = i * Q_BLOCK
        qb = lax.dynamic_slice_in_dim(qf, start, Q_BLOCK, axis=1)
        z = jnp.einsum('bqhd,bkhd->bhqk', qb, kf) * (d ** -0.5) + bias
        w = _sb_weights(z, start + jnp.arange(Q_BLOCK), k_pos)
        return jnp.einsum('bhqk,bkhd->bqhd', w, vf)

    o = lax.map(block, jnp.arange(t // Q_BLOCK))
    return jnp.moveaxis(o, 0, 1).reshape(b, t, h, d).astype(v.dtype)


def _sb_sample(q, k, v, past_k, past_v, bias):
    b, t, h, d = q.shape
    p_len = past_k.shape[1]
    qf = q.astype(jnp.float32)
    z = jnp.concatenate([
        jnp.einsum('bqhd,bkhd->bhqk', qf, past_k.astype(jnp.float32)),
        jnp.einsum('bqhd,bkhd->bhqk', qf, k.astype(jnp.float32))], axis=-1) * (d ** -0.5)
    z = z + bias.astype(jnp.float32)[None, :, None, None]
    w = _sb_weights(z, p_len + jnp.arange(t), jnp.arange(p_len + t))
    o = (jnp.einsum('bhqk,bkhd->bqhd', w[..., :p_len], past_v.astype(jnp.float32))
         + jnp.einsum('bhqk,bkhd->bqhd', w[..., p_len:], v.astype(jnp.float32)))
    return o.astype(v.dtype)


def _rwkv7_scan(r, w, k, v, a, bvec, s0):
    def step(s, inp):
        r_t, w_t, k_t, v_t, a_t, b_t = inp
        sa = jnp.einsum('bhij,bhj->bhi', s, a_t)
        s = (s * w_t[:, :, None, :] + sa[..., None] * b_t[:, :, None, :]
             + v_t[..., None] * k_t[:, :, None, :])
        return s, jnp.einsum('bhij,bhj->bhi', s, r_t)

    xs = tuple(jnp.moveaxis(x, 1, 0) for x in (r, w, k, v, a, bvec))
    s, ys = lax.scan(step, s0, xs)
    return jnp.moveaxis(ys, 0, 1), s


def _rwkv7(cols, shift_prev, s0, mu, w0, w2, a0, a2, g2, k_k, k_a, r_k, ln_w, ln_b):
    b, t, _ = cols.shape
    f32 = jnp.float32
    prev = jnp.concatenate([shift_prev[:, None].astype(cols.dtype), cols[:, :-1]], axis=1)
    mixed = cols + mu * (prev - cols)
    r, k, v, wl, al, gl = _split(mixed, (W_B, W_B, W_B, LORA_W, LORA_A, LORA_G))
    w_log = -jax.nn.softplus(-(w0 + jnp.tanh(wl) @ w2).astype(f32)) - 0.5
    decay = jnp.exp(-jnp.exp(w_log))
    a = jax.nn.sigmoid((a0 + al @ a2).astype(f32))
    g = jax.nn.sigmoid(gl) @ g2

    def heads(x):
        return x.astype(f32).reshape(b, t, H_B, D_HEAD_B)

    kk = _l2norm(heads(k * k_k))
    k = heads(k.astype(f32) * (1.0 + (a - 1.0) * k_a))
    r, v, a, decay = heads(r), heads(v), heads(a), heads(decay)
    y, s = _rwkv7_scan(r, decay, k, v, -kk, kk * a, s0.astype(f32))
    mean = jnp.mean(y, axis=-1, keepdims=True)
    var = jnp.mean(jnp.square(y - mean), axis=-1, keepdims=True)
    y = ((y - mean) * lax.rsqrt(var + LN_X_EPS)).reshape(b, t, W_B) * ln_w + ln_b
    bonus = (jnp.sum(r * k * r_k, axis=-1, keepdims=True) * v).reshape(b, t, W_B)
    out = (y + bonus) * g
    return out.astype(cols.dtype), s, cols[:, -1]


def _gated_delta_chunked(q, k, v, g, beta, s0, chunk):
    b, t, h, _ = q.shape
    dv = v.shape[-1]
    n = t // chunk

    def blocks(x):
        x = x.reshape((b, n, chunk, h) + x.shape[3:])
        return jnp.moveaxis(jnp.moveaxis(x, 3, 2), 1, 0)

    qc, kc, vc, gc, bc = (blocks(x) for x in (q, k, v, g, beta))
    cum = jnp.cumsum(gc, axis=-1)
    idx = jnp.arange(chunk)
    incl = idx[:, None] >= idx[None, :]
    strict = idx[:, None] > idx[None, :]
    diff = cum[..., :, None] - cum[..., None, :]
    decay = jnp.where(incl, jnp.exp(jnp.where(incl, diff, 0.0)), 0.0)
    kb = kc * bc[..., None]
    a_mat = jnp.where(strict, jnp.einsum('nbhid,nbhjd->nbhij', kb, kc) * decay, 0.0)
    t_mat = a_mat + jnp.eye(chunk, dtype=a_mat.dtype)
    u = lax.linalg.triangular_solve(t_mat, vc * bc[..., None], left_side=True, lower=True, unit_diagonal=True)
    w = lax.linalg.triangular_solve(t_mat, kb * jnp.exp(cum)[..., None], left_side=True, lower=True,
                                    unit_diagonal=True)
    qk = jnp.einsum('nbhid,nbhjd->nbhij', qc, kc) * decay
    q_dec = qc * jnp.exp(cum)[..., None]
    g_last = cum[..., -1]
    k_dec = kc * jnp.exp(g_last[..., None] - cum)[..., None]

    def step(s, inp):
        u_c, w_c, qk_c, qd_c, kd_c, gl_c = inp
        v_new = u_c - jnp.einsum('bhcd,bhde->bhce', w_c, s)
        o = jnp.einsum('bhcd,bhde->bhce', qd_c, s) + jnp.einsum('bhij,bhje->bhie', qk_c, v_new)
        s = s * jnp.exp(gl_c)[..., None, None] + jnp.einsum('bhcd,bhce->bhde', kd_c, v_new)
        return s, o

    s, o = lax.scan(step, s0, (u, w, qk, q_dec, k_dec, g_last))
    o = jnp.swapaxes(jnp.moveaxis(o, 0, 1), 2, 3).reshape(b, t, h, dv)
    return o, s


def _gated_deltanet(qkv, ab, z, conv_prev, s0, conv_w, a_log, dt_bias, norm_w, chunk):
    b, t, _ = qkv.shape
    f32 = jnp.float32
    full = jnp.concatenate([conv_prev.astype(qkv.dtype), qkv], axis=1)
    conv = jax.nn.silu(sum(full[:, i:i + t] * conv_w[i] for i in range(CONV_W)))
    new_conv = full[:, t:]
    q, k, v = (x.reshape(b, t, H_C, D_HEAD_C) for x in jnp.split(conv, 3, axis=-1))
    q = _l2norm(q) * (D_HEAD_C ** -0.5)
    k = _l2norm(k)
    alpha, b_raw = jnp.split(ab.astype(f32), 2, axis=-1)
    g = -jnp.exp(a_log.astype(f32)) * jax.nn.softplus(alpha + dt_bias.astype(f32))
    beta = jax.nn.sigmoid(b_raw)
    o, s = _gated_delta_chunked(q, k, v.astype(f32), g, beta, s0.astype(f32), chunk)
    o = o * lax.rsqrt(jnp.mean(o * o, axis=-1, keepdims=True) + NORM_EPS) * norm_w
    o = o * jax.nn.silu(z.reshape(b, t, H_C, D_HEAD_C).astype(f32))
    return o.reshape(b, t, W_C).astype(qkv.dtype), s, new_conv


def _layer(x, p, past_k, past_v, rwkv_state, rwkv_shift, gdn_state, gdn_conv, gdn_chunk):
    b, t, _ = x.shape
    x = x + 0.5 * _swiglu(_rmsnorm(x, p['ffn1_norm']), p['ffn1_w_gate'], p['ffn1_w_up'], p['ffn1_w_down'])
    h = _rmsnorm(x, p['mix_norm'])
    a_qkv, b_cols, c_qkv, c_ab, c_z, gates = _split(h @ p['w_in'], PROJ_SIZES)
    qa, ka, va = (u.reshape(b, t, H_A, D_HEAD_A) for u in jnp.split(a_qkv, 3, axis=-1))
    if past_k is None:
        oa = _sb_prompt(qa, ka, va, p['sb_bias'])
    else:
        oa = _sb_sample(qa, ka, va, past_k, past_v, p['sb_bias'])
    ob, rwkv_state, rwkv_shift = _rwkv7(
        b_cols, rwkv_shift, rwkv_state, p['rwkv_mu'], p['rwkv_w0'], p['rwkv_w2'], p['rwkv_a0'], p['rwkv_a2'],
        p['rwkv_g2'], p['rwkv_k_k'], p['rwkv_k_a'], p['rwkv_r_k'], p['rwkv_ln_w'], p['rwkv_ln_b'])
    oc, gdn_state, gdn_conv = _gated_deltanet(
        c_qkv, c_ab, c_z, gdn_conv, gdn_state, p['gdn_conv_w'], p['gdn_a_log'], p['gdn_dt_bias'],
        p['gdn_norm_w'], gdn_chunk)
    g_a, g_b, g_c = jnp.split(jax.nn.sigmoid(gates), 3, axis=-1)
    merged = (g_a * (oa.reshape(b, t, W_A) @ p['w_br_a'])
              + g_b * (ob @ p['w_br_b'])
              + g_c * (oc @ p['w_br_c']))
    x = x + merged @ p['w_out']
    x = x + 0.5 * _swiglu(_rmsnorm(x, p['ffn2_norm']), p['ffn2_w_gate'], p['ffn2_w_up'], p['ffn2_w_down'])
    return x, ka, va, rwkv_state, rwkv_shift, gdn_state, gdn_conv


def setup_inputs(seed: int = 0) -> dict:
    key = jax.random.key(seed)
    keys = list(jax.random.split(key, 64))

    def take():
        return keys.pop()

    def nrm(shape, scale):
        return scale * jax.random.normal(take(), shape, jnp.float32)

    def gain(shape):
        return 1.0 + 0.02 * jax.random.normal(take(), shape, jnp.float32)

    def unif(shape, lo, hi):
        return jax.random.uniform(take(), shape, jnp.float32, lo, hi)

    L, D = DEPTH, D_MODEL
    n_pages = PAST_LEN // PAGE_SIZE
    n_used = DEC_BATCH * n_pages
    n_pool = n_used + n_used // 4
    page_table = jax.random.permutation(take(), n_pool)[:n_used].reshape(DEC_BATCH, n_pages).astype(jnp.int32)
    dt = jnp.exp(unif((L, H_C), math.log(1e-3), math.log(1e-1)))
    return {
        'x_prompt': nrm((BATCH, SEQ, D), 1.0),
        'x_sample': nrm((DEC_BATCH, DEC_SEQ, D), 1.0),
        'cache_k': nrm((n_pool, L, PAGE_SIZE, H_A, D_HEAD_A), 1.0),
        'cache_v': nrm((n_pool, L, PAGE_SIZE, H_A, D_HEAD_A), 1.0),
        'page_table': page_table,
        'state_rwkv': nrm((L, DEC_BATCH, H_B, D_HEAD_B, D_HEAD_B), 0.1),
        'state_rwkv_shift': nrm((L, DEC_BATCH, R_B), 1.0),
        'state_gdn': nrm((L, DEC_BATCH, H_C, D_HEAD_C, D_HEAD_C), 0.1),
        'state_gdn_conv': nrm((L, DEC_BATCH, CONV_W - 1, 3 * W_C), 1.0),
        'ffn1_norm': gain((L, D)),
        'ffn1_w_gate': nrm((L, D, D_FF), D ** -0.5),
        'ffn1_w_up': nrm((L, D, D_FF), D ** -0.5),
        'ffn1_w_down': nrm((L, D_FF, D), D_FF ** -0.5),
        'mix_norm': gain((L, D)),
        'w_in': nrm((L, D, PROJ_W), D ** -0.5),
        'sb_bias': unif((L, H_A), -8.0, -6.0),
        'rwkv_mu': unif((L, R_B), 0.0, 1.0),
        'rwkv_w0': unif((L, W_B), -5.0, -0.5),
        'rwkv_w2': nrm((L, LORA_W, W_B), 0.5 * LORA_W ** -0.5),
        'rwkv_a0': nrm((L, W_B), 0.1),
        'rwkv_a2': nrm((L, LORA_A, W_B), 0.5 * LORA_A ** -0.5),
        'rwkv_g2': nrm((L, LORA_G, W_B), LORA_G ** -0.5),
        'rwkv_k_k': 0.85 + nrm((L, W_B), 0.02),
        'rwkv_k_a': gain((L, W_B)),
        'rwkv_r_k': nrm((L, H_B, D_HEAD_B), 0.1),
        'rwkv_ln_w': gain((L, W_B)),
        'rwkv_ln_b': nrm((L, W_B), 0.02),
        'gdn_conv_w': nrm((L, CONV_W, 3 * W_C), CONV_W ** -0.5),
        'gdn_a_log': jnp.log(unif((L, H_C), 1.0, 16.0)),
        'gdn_dt_bias': jnp.log(jnp.expm1(dt)),
        'gdn_norm_w': gain((L, D_HEAD_C)),
        'w_br_a': nrm((L, W_A, D), W_A ** -0.5),
        'w_br_b': nrm((L, W_B, D), W_B ** -0.5),
        'w_br_c': nrm((L, W_C, D), W_C ** -0.5),
        'w_out': nrm((L, D, D), D ** -0.5),
        'ffn2_norm': gain((L, D)),
        'ffn2_w_gate': nrm((L, D, D_FF), D ** -0.5),
        'ffn2_w_up': nrm((L, D, D_FF), D ** -0.5),
        'ffn2_w_down': nrm((L, D_FF, D), D_FF ** -0.5),
        'final_norm': gain((D,)),
    }


def reference(x_prompt, x_sample, cache_k, cache_v, page_table, state_rwkv, state_rwkv_shift, state_gdn,
              state_gdn_conv, ffn1_norm, ffn1_w_gate, ffn1_w_up, ffn1_w_down, mix_norm, w_in, sb_bias,
              rwkv_mu, rwkv_w0, rwkv_w2, rwkv_a0, rwkv_a2, rwkv_g2, rwkv_k_k, rwkv_k_a, rwkv_r_k,
              rwkv_ln_w, rwkv_ln_b, gdn_conv_w, gdn_a_log, gdn_dt_bias, gdn_norm_w,
              w_br_a, w_br_b, w_br_c, w_out, ffn2_norm, ffn2_w_gate, ffn2_w_up, ffn2_w_down, final_norm):
    f32 = jnp.float32
    xp, xs = x_prompt, x_sample
    bp, bs = xp.shape[0], xs.shape[0]
    kp_l, vp_l, ks_l, vs_l = [], [], [], []
    rp_l, rs_l, shp_l, shs_l = [], [], [], []
    gp_l, gs_l, cp_l, cs_l = [], [], [], []
    for l in range(DEPTH):
        p = dict(
            ffn1_norm=ffn1_norm[l], ffn1_w_gate=ffn1_w_gate[l], ffn1_w_up=ffn1_w_up[l], ffn1_w_down=ffn1_w_down[l],
            mix_norm=mix_norm[l], w_in=w_in[l], sb_bias=sb_bias[l],
            rwkv_mu=rwkv_mu[l], rwkv_w0=rwkv_w0[l], rwkv_w2=rwkv_w2[l], rwkv_a0=rwkv_a0[l], rwkv_a2=rwkv_a2[l],
            rwkv_g2=rwkv_g2[l], rwkv_k_k=rwkv_k_k[l], rwkv_k_a=rwkv_k_a[l], rwkv_r_k=rwkv_r_k[l],
            rwkv_ln_w=rwkv_ln_w[l], rwkv_ln_b=rwkv_ln_b[l],
            gdn_conv_w=gdn_conv_w[l], gdn_a_log=gdn_a_log[l], gdn_dt_bias=gdn_dt_bias[l], gdn_norm_w=gdn_norm_w[l],
            w_br_a=w_br_a[l], w_br_b=w_br_b[l], w_br_c=w_br_c[l], w_out=w_out[l],
            ffn2_norm=ffn2_norm[l], ffn2_w_gate=ffn2_w_gate[l], ffn2_w_up=ffn2_w_up[l], ffn2_w_down=ffn2_w_down[l])
        xp, kp, vp, rp, shp, gp, cp = _layer(
            xp, p, None, None,
            jnp.zeros((bp, H_B, D_HEAD_B, D_HEAD_B), f32), jnp.zeros((bp, R_B), xp.dtype),
            jnp.zeros((bp, H_C, D_HEAD_C, D_HEAD_C), f32), jnp.zeros((bp, CONV_W - 1, 3 * W_C), xp.dtype),
            GDN_CHUNK)
        past_k = cache_k[page_table, l].reshape(bs, -1, H_A, D_HEAD_A)
        past_v = cache_v[page_table, l].reshape(bs, -1, H_A, D_HEAD_A)
        xs, ks, vs, rs, shs, gs, cs = _layer(
            xs, p, past_k, past_v, state_rwkv[l], state_rwkv_shift[l], state_gdn[l], state_gdn_conv[l],
            xs.shape[1])
        kp_l.append(kp); vp_l.append(vp); ks_l.append(ks); vs_l.append(vs)
        rp_l.append(rp); rs_l.append(rs); shp_l.append(shp); shs_l.append(shs)
        gp_l.append(gp); gs_l.append(gs); cp_l.append(cp); cs_l.append(cs)
    y_prompt = _rmsnorm(xp, final_norm)
    y_sample = _rmsnorm(xs, final_norm)
    return (y_prompt, y_sample,
            jnp.stack(kp_l, axis=1), jnp.stack(vp_l, axis=1), jnp.stack(ks_l, axis=1), jnp.stack(vs_l, axis=1),
            jnp.stack(rp_l), jnp.stack(rs_l), jnp.stack(shp_l), jnp.stack(shs_l),
            jnp.stack(gp_l), jnp.stack(gs_l), jnp.stack(cp_l), jnp.stack(cs_l))
```

```python
import functools

import jax
import jax.numpy as jnp
from jax import lax
from jax.experimental import pallas as pl
from jax.experimental.pallas import tpu as pltpu

F32 = jnp.float32
BF16 = jnp.bfloat16

NORM_EPS = 1e-6
L2_EPS = 1e-6
H_A, D_A = 8, 64
W_A = H_A * D_A
H_B, D_B = 8, 64
W_B = H_B * D_B
LORA_W, LORA_A, LORA_G = 64, 64, 128
R_B = 3 * W_B + LORA_W + LORA_A + LORA_G
LN_X_EPS = D_B * 1e-5
H_C, D_C = 4, 128
W_C = H_C * D_C
CONV_W = 4
CHUNK = 64
LANES = 128
SUBLANES = 8
VMEM_LIMIT_BYTES = 52 * 1024 * 1024


def _cparams(*sem):
    return pltpu.CompilerParams(dimension_semantics=sem, vmem_limit_bytes=VMEM_LIMIT_BYTES)


def _dot(a, b):
    return jnp.dot(a.astype(BF16), b.astype(BF16), preferred_element_type=F32)


def _dot_nt(a, b):
    return lax.dot_general(a.astype(BF16), b.astype(BF16), (((1,), (1,)), ((), ())),
                           preferred_element_type=F32)


def _dot_tn(a, b):
    return lax.dot_general(a.astype(BF16), b.astype(BF16), (((0,), (0,)), ((), ())),
                           preferred_element_type=F32)


def _split(x):
    hi = x.astype(BF16)
    lo = (x - hi.astype(F32)).astype(BF16)
    return hi, lo


def _dot_x3(a, b):
    ah, al = _split(a)
    bh, bl = _split(b)
    return (jnp.dot(ah, bh, preferred_element_type=F32)
            + jnp.dot(ah, bl, preferred_element_type=F32)
            + jnp.dot(al, bh, preferred_element_type=F32))


def _dot_exact_rhs(a, b_bf16):
    ah, al = _split(a)
    return (jnp.dot(ah, b_bf16, preferred_element_type=F32)
            + jnp.dot(al, b_bf16, preferred_element_type=F32))


def _dot_exact_lhs(a_bf16, b):
    bh, bl = _split(b)
    return (jnp.dot(a_bf16, bh, preferred_element_type=F32)
            + jnp.dot(a_bf16, bl, preferred_element_type=F32))


def _softplus(z):
    return jnp.maximum(z, 0.0) + jnp.log(1.0 + jnp.exp(-jnp.abs(z)))


def _sigmoid(z):
    return 1.0 / (1.0 + jnp.exp(-z))


def _rms(x, w):
    return x * lax.rsqrt(jnp.mean(x * x, axis=-1, keepdims=True) + NORM_EPS) * w


def _tri_inverse(l_mat, n_levels):
    n = l_mat.shape[0]
    eye = (lax.broadcasted_iota(jnp.int32, (n, n), 0)
           == lax.broadcasted_iota(jnp.int32, (n, n), 1)).astype(F32)
    inv = eye + l_mat
    p = l_mat
    for _ in range(n_levels - 1):
        p = _dot_x3(p, p)
        inv = inv + _dot_x3(inv, p)
    return inv


def _ffn_body(x_ref, nw_ref, wg_ref, wu_ref, wd_ref, o_ref, hn_ref, acc_ref):
    j = pl.program_id(1)

    @pl.when(j == 0)
    def _():
        hn_ref[...] = _rms(x_ref[...], nw_ref[...]).astype(BF16)
        acc_ref[...] = jnp.zeros_like(acc_ref)

    h = hn_ref[...]
    g = jnp.dot(h, wg_ref[...], preferred_element_type=F32)
    u = jnp.dot(h, wu_ref[...], preferred_element_type=F32)
    act = (g * _sigmoid(g) * u).astype(BF16)
    acc_ref[...] += jnp.dot(act, wd_ref[...], preferred_element_type=F32)

    @pl.when(j == pl.num_programs(1) - 1)
    def _():
        o_ref[...] = x_ref[...] + 0.5 * acc_ref[...]


def _ffn(x, nw, wg, wu, wd, l, tm):
    m, d = x.shape
    ff = wg.shape[-1]
    tm = min(tm, m)
    tf = ff // 2 if (ff // 2) % LANES == 0 else ff
    return pl.pallas_call(
        _ffn_body,
        grid=(m // tm, ff // tf),
        in_specs=[
            pl.BlockSpec((tm, d), lambda i, j: (i, 0)),
            pl.BlockSpec((None, 1, d), lambda i, j: (l, 0, 0)),
            pl.BlockSpec((None, d, tf), lambda i, j: (l, 0, j)),
            pl.BlockSpec((None, d, tf), lambda i, j: (l, 0, j)),
            pl.BlockSpec((None, tf, d), lambda i, j: (l, j, 0)),
        ],
        out_specs=pl.BlockSpec((tm, d), lambda i, j: (i, 0)),
        out_shape=jax.ShapeDtypeStruct((m, d), F32),
        scratch_shapes=[pltpu.VMEM((tm, d), BF16), pltpu.VMEM((tm, d), F32)],
        compiler_params=_cparams("parallel", "arbitrary"),
        name="ffn",
    )(x, nw, wg, wu, wd)


def _proj_body(x_ref, nw_ref, w_ref, *o_refs, widths, sigmoid_last):
    h = _rms(x_ref[...], nw_ref[...]).astype(BF16)
    off = 0
    for idx, (o_ref, wd) in enumerate(zip(o_refs, widths)):
        y = jnp.dot(h, w_ref[:, off:off + wd], preferred_element_type=F32)
        if sigmoid_last and idx == len(widths) - 1:
            y = _sigmoid(y)
        o_ref[...] = y
        off += wd


def _proj(x, nw, w, l, widths, tm, sigmoid_last=False):
    m, d = x.shape
    n = w.shape[-1]
    assert sum(widths) == n
    tm = min(tm, m)
    return pl.pallas_call(
        functools.partial(_proj_body, widths=widths, sigmoid_last=sigmoid_last),
        grid=(m // tm,),
        in_specs=[
            pl.BlockSpec((tm, d), lambda i: (i, 0)),
            pl.BlockSpec((None, 1, d), lambda i: (l, 0, 0)),
            pl.BlockSpec((None, d, n), lambda i: (l, 0, 0)),
        ],
        out_specs=[pl.BlockSpec((tm, wd), lambda i: (i, 0)) for wd in widths],
        out_shape=[jax.ShapeDtypeStruct((m, wd), F32) for wd in widths],
        compiler_params=_cparams("parallel"),
        name="proj",
    )(x, nw, w)


def _merge_body(x_ref, oa_ref, ob_ref, oc_ref, ga_ref, gb_ref, gc_ref,
                wa_ref, wb_ref, wc_ref, wo_ref, o_ref):
    merged = (ga_ref[...] * _dot(oa_ref[...], wa_ref[...])
              + gb_ref[...] * _dot(ob_ref[...], wb_ref[...])
              + gc_ref[...] * _dot(oc_ref[...], wc_ref[...]))
    o_ref[...] = x_ref[...] + _dot(merged, wo_ref[...])


def _merge(x, oa, ob, oc, gates, wa, wb, wc, wo, l, tm):
    m, d = x.shape
    tm = min(tm, m)
    row = lambda i: (i, 0)
    wspec = lambda k: pl.BlockSpec((None, k, d), lambda i: (l, 0, 0))
    return pl.pallas_call(
        _merge_body,
        grid=(m // tm,),
        in_specs=[
            pl.BlockSpec((tm, d), row),
            pl.BlockSpec((tm, W_A), row), pl.BlockSpec((tm, W_B), row), pl.BlockSpec((tm, W_C), row),
            pl.BlockSpec((tm, d), lambda i: (i, 0)),
            pl.BlockSpec((tm, d), lambda i: (i, 1)),
            pl.BlockSpec((tm, d), lambda i: (i, 2)),
            wspec(W_A), wspec(W_B), wspec(W_C), wspec(d),
        ],
        out_specs=pl.BlockSpec((tm, d), row),
        out_shape=jax.ShapeDtypeStruct((m, d), F32),
        compiler_params=_cparams("parallel"),
        name="merge",
    )(x, oa, ob, oc, gates, gates, gates, wa, wb, wc, wo)


def _norm_body(x_ref, w_ref, o_ref):
    o_ref[...] = _rms(x_ref[...], w_ref[...])


def _final_norm(x, w, tm):
    m, d = x.shape
    tm = min(tm, m)
    return pl.pallas_call(
        _norm_body,
        grid=(m // tm,),
        in_specs=[pl.BlockSpec((tm, d), lambda i: (i, 0)), pl.BlockSpec((1, d), lambda i: (0, 0))],
        out_specs=pl.BlockSpec((tm, d), lambda i: (i, 0)),
        out_shape=jax.ShapeDtypeStruct((m, d), F32),
        compiler_params=_cparams("parallel"),
        name="final_norm",
    )(x, w)


def _sb_tile(z, vblk, o, r, u2, mask):
    tk = u2.shape[0]
    sp = _softplus(z)
    if mask is not None:
        sp = jnp.where(mask, sp, 0.0)
    cs = _dot_exact_rhs(sp, u2)
    a = jnp.exp(z - (r + cs[:, :tk]))
    if mask is not None:
        a = jnp.where(mask, a, 0.0)
    o = o + jnp.dot(a.astype(BF16), vblk, preferred_element_type=F32)
    return o, r + cs[:, tk:]


def _suffix_sum_matrix(tk):
    ri = lax.broadcasted_iota(jnp.int32, (tk, tk), 0)
    ci = lax.broadcasted_iota(jnp.int32, (tk, tk), 1)
    tri = jnp.where(ri >= ci, 1.0, 0.0)
    return jnp.concatenate([tri, jnp.ones((tk, tk), F32)], axis=1).astype(BF16)


def _attn_prompt_body(bias_ref, q_ref, k_ref, v_ref, o_ref, kb_ref, vb_ref, *, tq):
    p = pl.program_id(1)
    qi = pl.program_id(2)

    @pl.when(qi == 0)
    def _():
        kb_ref[...] = k_ref[...].astype(BF16)
        vb_ref[...] = v_ref[...].astype(BF16)

    q = q_ref[...] * (D_A ** -0.5)
    lane = lax.broadcasted_iota(jnp.int32, (tq, LANES), 1)
    ri = lax.broadcasted_iota(jnp.int32, (tq, tq), 0)
    ci = lax.broadcasted_iota(jnp.int32, (tq, tq), 1)
    causal = ci < ri
    u2 = _suffix_sum_matrix(tq)
    outs = []
    for hh in range(LANES // D_A):
        qh = jnp.where((lane >= D_A * hh) & (lane < D_A * (hh + 1)), q, 0.0).astype(BF16)
        bias = bias_ref[(LANES // D_A) * p + hh]

        def tile(j, o, r, mask, qh=qh, bias=bias):
            start = pl.multiple_of(j * tq, tq)
            kblk = kb_ref[pl.ds(start, tq), :]
            vblk = vb_ref[pl.ds(start, tq), :]
            z = _dot_nt(qh, kblk) + bias
            return _sb_tile(z, vblk, o, r, u2, mask)

        o, r = tile(qi, jnp.zeros((tq, LANES), F32), jnp.zeros((tq, tq), F32), causal)
        o, r = lax.fori_loop(0, qi, lambda jj, c, tile=tile: tile(qi - 1 - jj, c[0], c[1], None), (o, r))
        outs.append(o)
    o_ref[...] = jnp.where(lane < D_A, outs[0], outs[1])


def _attn_prompt(q, k, v, bias, b, t, tq=128):
    m = b * t
    nq = t // tq
    npair = W_A // LANES
    grid_spec = pltpu.PrefetchScalarGridSpec(
        num_scalar_prefetch=1,
        grid=(b, npair, nq),
        in_specs=[
            pl.BlockSpec((tq, LANES), lambda bi, p, qi, bias: (bi * nq + qi, p)),
            pl.BlockSpec((t, LANES), lambda bi, p, qi, bias: (bi, p)),
            pl.BlockSpec((t, LANES), lambda bi, p, qi, bias: (bi, p)),
        ],
        out_specs=pl.BlockSpec((tq, LANES), lambda bi, p, qi, bias: (bi * nq + qi, p)),
        scratch_shapes=[pltpu.VMEM((t, LANES), BF16), pltpu.VMEM((t, LANES), BF16)],
    )
    return pl.pallas_call(
        functools.partial(_attn_prompt_body, tq=tq),
        grid_spec=grid_spec,
        out_shape=jax.ShapeDtypeStruct((m, W_A), F32),
        compiler_params=_cparams("parallel", "parallel", "arbitrary"),
        name="attn_prompt",
    )(bias, q, k, v)


def _attn_sample_body(pt_ref, bias_ref, q_ref, kn_ref, vn_ref, *rest, n_pages, page):
    kp_refs = rest[:n_pages]
    vp_refs = rest[n_pages:2 * n_pages]
    o_ref = rest[2 * n_pages]
    nrow = H_A * SUBLANES
    q8 = q_ref[...] * (D_A ** -0.5)
    head_of_lane = lax.broadcasted_iota(jnp.int32, (SUBLANES, W_A), 1) // D_A
    qm = jnp.concatenate([jnp.where(head_of_lane == h, q8, 0.0) for h in range(H_A)], axis=0).astype(BF16)
    ri = lax.broadcasted_iota(jnp.int32, (nrow, page), 0)
    ci = lax.broadcasted_iota(jnp.int32, (nrow, page), 1)
    bias = jnp.zeros((nrow, page), F32)
    for h in range(H_A):
        bias = jnp.where(ri // SUBLANES == h, bias_ref[h], bias)
    mask_new = ci < (ri % SUBLANES)
    u2 = _suffix_sum_matrix(page)

    def tile(kblk, vblk, o, r, mask):
        z = _dot_nt(qm, kblk) + bias
        return _sb_tile(z, vblk.astype(BF16), o, r, u2, mask)

    pad = jnp.zeros((page - SUBLANES, W_A), F32)
    k_new = jnp.concatenate([kn_ref[...], pad], axis=0)
    v_new = jnp.concatenate([vn_ref[...], pad], axis=0)
    o, r = tile(k_new, v_new, jnp.zeros((nrow, W_A), F32), jnp.zeros((nrow, page), F32), mask_new)
    for j in reversed(range(n_pages)):
        o, r = tile(kp_refs[j][...], vp_refs[j][...], o, r, None)
    out = jnp.zeros((SUBLANES, W_A), F32)
    for h in range(H_A):
        out = jnp.where(head_of_lane == h, o[h * SUBLANES:(h + 1) * SUBLANES, :], out)
    o_ref[...] = out


def _attn_sample(q8, k8, v8, cache_k, cache_v, page_table, bias, l):
    bs = q8.shape[0]
    n_pages = page_table.shape[1]
    page = cache_k.shape[2]
    new_spec = pl.BlockSpec((None, SUBLANES, W_A), lambda bi, pt, bias: (bi, 0, 0))

    def page_spec(j):
        return pl.BlockSpec((None, None, page, W_A), lambda bi, pt, bias, j=j: (pt[bi, j], l, 0, 0))

    grid_spec = pltpu.PrefetchScalarGridSpec(
        num_scalar_prefetch=2,
        grid=(bs,),
        in_specs=[new_spec, new_spec, new_spec]
        + [page_spec(j) for j in range(n_pages)] + [page_spec(j) for j in range(n_pages)],
        out_specs=new_spec,
    )
    return pl.pallas_call(
        functools.partial(_attn_sample_body, n_pages=n_pages, page=page),
        grid_spec=grid_spec,
        out_shape=jax.ShapeDtypeStruct((bs, SUBLANES, W_A), F32),
        compiler_params=_cparams("parallel"),
        name="attn_sample",
    )(page_table, bias, q8, k8, v8, *([cache_k] * n_pages), *([cache_v] * n_pages))


def _rwkv_body(cols_ref, shift_ref, s0_ref, mu_ref, w0_ref, wwa_ref, a0_ref, g2_ref, kk_ref, ka_ref,
               rk_ref, lnw_ref, lnb_ref, ones_ref, ob_ref, sout_ref, s_ref, prev_ref, *, t_valid, padded):
    c = pl.program_id(1)
    nc = pl.num_programs(1)
    ch = cols_ref.shape[0]
    grp = 4 * D_B
    n_grp = W_B // grp

    @pl.when(c == 0)
    def _():
        prev_ref[0:1, :] = shift_ref[...]
        s_ref[...] = jnp.zeros_like(s_ref)
        for h in range(H_B):
            g, hh = divmod(h, 4)
            s_ref[g, hh * D_B:(hh + 1) * D_B, hh * D_B:(hh + 1) * D_B] = s0_ref[h]

    cols = cols_ref[...]
    row = lax.broadcasted_iota(jnp.int32, (ch, 1), 0)
    prev = jnp.where(row == 0, prev_ref[0:1, :], pltpu.roll(cols, 1, 0))
    prev_ref[0:1, :] = cols[ch - 1:ch, :]
    mixed = cols + mu_ref[...] * (prev - cols)
    r = mixed[:, 0:W_B]
    k = mixed[:, W_B:2 * W_B]
    v = mixed[:, 2 * W_B:3 * W_B]
    lora = mixed[:, 3 * W_B:3 * W_B + LORA_W + LORA_A]
    gl = mixed[:, 3 * W_B + LORA_W + LORA_A:]
    lane = lax.broadcasted_iota(jnp.int32, lora.shape, 1)
    lora = jnp.where(lane < LORA_W, jnp.tanh(lora), lora)
    wa = _dot(lora, wwa_ref[...])
    w_log = -_softplus(-(w0_ref[...] + wa[:, :W_B])) - 0.5
    logw = -jnp.exp(w_log)
    a = _sigmoid(a0_ref[...] + wa[:, W_B:])
    g = _dot(_sigmoid(gl), g2_ref[...])
    ones_bd = ones_ref[...]
    kk = k * kk_ref[...]
    kk = kk * lax.rsqrt(_dot_exact_rhs(kk * kk, ones_bd) + L2_EPS)
    k2 = k * (1.0 + (a - 1.0) * ka_ref[...])
    v_s = v
    if padded:
        valid = row + c * ch < t_valid
        logw = jnp.where(valid, logw, 0.0)
        kk = jnp.where(valid, kk, 0.0)
        k2 = jnp.where(valid, k2, 0.0)
        v_s = jnp.where(valid, v, 0.0)
    a_s = -kk
    b_s = kk * a

    tr = lax.broadcasted_iota(jnp.int32, (ch, ch), 0)
    tc = lax.broadcasted_iota(jnp.int32, (ch, ch), 1)
    tri = jnp.where(tr >= tc, 1.0, 0.0).astype(BF16)
    cum = _dot_exact_lhs(tri, logw)
    e_in = jnp.exp(cum)
    e_neg = jnp.exp(-cum)
    at = a_s * jnp.exp(cum - logw)
    bt = b_s * e_neg
    kt = k2 * e_neg
    rt = r * e_in
    wc = e_in[ch - 1:ch, :]

    n = 4 * ch
    ri = lax.broadcasted_iota(jnp.int32, (n, grp), 0)
    ci = lax.broadcasted_iota(jnp.int32, (n, grp), 1)
    bm = (ri // ch) == (ci // D_B)
    pr = lax.broadcasted_iota(jnp.int32, (n, n), 0)
    pc = lax.broadcasted_iota(jnp.int32, (n, n), 1)
    same = (pr // ch) == (pc // ch)
    strict = same & (pc < pr)
    incl = same & (pc <= pr)
    n_levels = max(1, (ch - 1).bit_length())

    ys = []
    for gi in range(n_grp):
        sl = slice(gi * grp, (gi + 1) * grp)

        def bd(x, sl=sl):
            return jnp.where(bm, jnp.concatenate([x[:, sl]] * 4, axis=0), 0.0).astype(BF16)

        ab, bb, kb, rb, vb = bd(at), bd(bt), bd(kt), bd(rt), bd(v_s)
        s_old = s_ref[gi]
        sb = s_old.astype(BF16)
        l_ab = jnp.where(strict, _dot_nt(ab, bb), 0.0)
        l_ak = jnp.where(strict, _dot_nt(ab, kb), 0.0)
        m_rb = jnp.where(incl, _dot_nt(rb, bb), 0.0)
        m_rk = jnp.where(incl, _dot_nt(rb, kb), 0.0)
        rhs = _dot_nt(ab, sb) + _dot(l_ak, vb)
        u = _dot_x3(_tri_inverse(l_ab, n_levels), rhs)
        ub = u.astype(BF16)
        y = _dot_nt(rb, sb) + _dot(m_rb, ub) + _dot(m_rk, vb)
        ys.append(y[0:ch] + y[ch:2 * ch] + y[2 * ch:3 * ch] + y[3 * ch:4 * ch])
        s_ref[gi] = (s_old + _dot_tn(ub, bb) + _dot_tn(vb, kb)) * wc[:, sl]
    y = jnp.concatenate(ys, axis=1)

    inv_d = 1.0 / D_B
    mean = _dot_exact_rhs(y, ones_bd) * inv_d
    dev = y - mean
    var = _dot_exact_rhs(dev * dev, ones_bd) * inv_d
    yn = dev * lax.rsqrt(var + LN_X_EPS) * lnw_ref[...] + lnb_ref[...]
    bonus = _dot_exact_rhs(r * k2 * rk_ref[...], ones_bd) * v
    ob_ref[...] = (yn + bonus) * g

    @pl.when(c == nc - 1)
    def _():
        for h in range(H_B):
            gq, hh = divmod(h, 4)
            sout_ref[h] = s_ref[gq, hh * D_B:(hh + 1) * D_B, hh * D_B:(hh + 1) * D_B]


def _rwkv(cols, shift, s0, prm, l, t_valid):
    b, t, _ = cols.shape
    ch = CHUNK
    nc = t // ch
    vec = lambda n: pl.BlockSpec((None, 1, n), lambda bi, c: (l, 0, 0))
    mat = lambda k, n: pl.BlockSpec((None, k, n), lambda bi, c: (l, 0, 0))
    return pl.pallas_call(
        functools.partial(_rwkv_body, t_valid=t_valid, padded=t_valid < t),
        grid=(b, nc),
        in_specs=[
            pl.BlockSpec((None, ch, R_B), lambda bi, c: (bi, c, 0)),
            pl.BlockSpec((None, 1, R_B), lambda bi, c: (bi, 0, 0)),
            pl.BlockSpec((None, H_B, D_B, D_B), lambda bi, c: (bi, 0, 0, 0)),
            vec(R_B), vec(W_B), mat(LORA_W + LORA_A, 2 * W_B), vec(W_B), mat(LORA_G, W_B),
            vec(W_B), vec(W_B), vec(W_B), vec(W_B), vec(W_B),
            pl.BlockSpec((W_B, W_B), lambda bi, c: (0, 0)),
        ],
        out_specs=[
            pl.BlockSpec((None, ch, W_B), lambda bi, c: (bi, c, 0)),
            pl.BlockSpec((None, H_B, D_B, D_B), lambda bi, c: (bi, 0, 0, 0)),
        ],
        out_shape=[jax.ShapeDtypeStruct((b, t, W_B), F32),
                   jax.ShapeDtypeStruct((b, H_B, D_B, D_B), F32)],
        scratch_shapes=[pltpu.VMEM((W_B // (4 * D_B), 4 * D_B, 4 * D_B), F32),
                        pltpu.VMEM((SUBLANES, R_B), F32)],
        compiler_params=_cparams("parallel", "arbitrary"),
        name="rwkv7",
    )(cols, shift, s0, prm["mu"], prm["w0"], prm["wwa"], prm["a0"], prm["g2"], prm["k_k"], prm["k_a"],
      prm["r_k"], prm["ln_w"], prm["ln_b"], prm["ones_bd"])


def _gdn_body(x_ref, ab_ref, z_ref, cprev_ref, s0_ref, cw_ref, alog_ref, dtb_ref, nw_ref,
              oc_ref, sout_ref, s_ref, xbuf_ref, *, t_valid, padded):
    c = pl.program_id(1)
    nc = pl.num_programs(1)
    ch = x_ref.shape[0]

    @pl.when(c == 0)
    def _():
        xbuf_ref[0:SUBLANES, :] = cprev_ref[...]
        s_ref[...] = s0_ref[...]

    x = x_ref[...]
    xbuf_ref[SUBLANES:SUBLANES + ch, :] = x
    cw = cw_ref[...]
    conv = x * cw[CONV_W - 1:CONV_W, :]
    for i in range(1, CONV_W):
        conv = conv + xbuf_ref[SUBLANES - i:SUBLANES - i + ch, :] * cw[CONV_W - 1 - i:CONV_W - i, :]
    conv = conv * _sigmoid(conv)
    xbuf_ref[0:SUBLANES, :] = x[ch - SUBLANES:ch, :]

    def stack(y):
        return jnp.concatenate([y[:, h * D_C:(h + 1) * D_C] for h in range(H_C)], axis=0)

    q = stack(conv[:, 0:W_C])
    k = stack(conv[:, W_C:2 * W_C])
    v = stack(conv[:, 2 * W_C:3 * W_C])
    q = q * lax.rsqrt(jnp.sum(q * q, axis=-1, keepdims=True) + L2_EPS) * (D_C ** -0.5)
    k = k * lax.rsqrt(jnp.sum(k * k, axis=-1, keepdims=True) + L2_EPS)

    ab = ab_ref[...]
    g_all = -jnp.exp(alog_ref[...]) * _softplus(ab + dtb_ref[...])
    beta_all = _sigmoid(ab)
    if padded:
        row = lax.broadcasted_iota(jnp.int32, (ch, 1), 0)
        valid = row + c * ch < t_valid
        g_all = jnp.where(valid, g_all, 0.0)
        beta_all = jnp.where(valid, beta_all, 0.0)
    tr = lax.broadcasted_iota(jnp.int32, (ch, ch), 0)
    tc = lax.broadcasted_iota(jnp.int32, (ch, ch), 1)
    tri = jnp.where(tr >= tc, 1.0, 0.0).astype(BF16)
    cum_all = _dot_exact_lhs(tri, g_all)

    def rows(src, lane0):
        return jnp.concatenate(
            [jnp.broadcast_to(src[:, lane0 + h:lane0 + h + 1], (src.shape[0], D_C)) for h in range(H_C)], axis=0)

    cum = rows(cum_all, 0)
    beta = rows(beta_all, H_C)
    g_last = jnp.concatenate(
        [jnp.broadcast_to(cum_all[ch - 1:ch, h:h + 1], (ch, D_C)) for h in range(H_C)], axis=0)

    n = H_C * ch
    pr = lax.broadcasted_iota(jnp.int32, (n, n), 0)
    pc = lax.broadcasted_iota(jnp.int32, (n, n), 1)
    same = (pr // ch) == (pc // ch)
    strict = same & (pc < pr)
    incl = same & (pc <= pr)
    cum_col = jnp.concatenate([cum] * (n // D_C), axis=1)
    cum_row = jnp.broadcast_to(jnp.transpose(cum)[0:1, :], (n, n))
    decay = jnp.where(incl, jnp.exp(jnp.where(incl, cum_col - cum_row, 0.0)), 0.0)

    kbeta = k * beta
    kb16 = k.astype(BF16)
    a_mat = jnp.where(strict, _dot_nt(kbeta, kb16) * decay, 0.0)
    t_inv = _tri_inverse(-a_mat, max(1, (ch - 1).bit_length()))
    e_cum = jnp.exp(cum)
    u_cap = _dot_x3(t_inv, v * beta)
    w_cap = _dot_x3(t_inv, kbeta * e_cum)
    qk = _dot_nt(q, kb16) * decay
    q_dec = q * e_cum
    k_dec = k * jnp.exp(g_last - cum)

    v_new, o_state = [], []
    for h in range(H_C):
        sl = slice(h * ch, (h + 1) * ch)
        s_old = s_ref[h]
        sb = s_old.astype(BF16)
        vn = u_cap[sl] - _dot(w_cap[sl], sb)
        o_state.append(_dot(q_dec[sl], sb))
        v_new.append(vn)
        s_ref[h] = s_old * jnp.exp(g_last[h * ch:h * ch + 1, :]) + _dot_tn(k_dec[sl], vn)
    o = jnp.concatenate(o_state, axis=0) + _dot(qk, jnp.concatenate(v_new, axis=0))

    z = z_ref[...]
    for h in range(H_C):
        oh = o[h * ch:(h + 1) * ch]
        oh = oh * lax.rsqrt(jnp.mean(oh * oh, axis=-1, keepdims=True) + NORM_EPS) * nw_ref[...]
        zh = z[:, h * D_C:(h + 1) * D_C]
        oc_ref[:, h * D_C:(h + 1) * D_C] = oh * (zh * _sigmoid(zh))

    @pl.when(c == nc - 1)
    def _():
        sout_ref[...] = s_ref[...]


def _gdn(x, ab, z, cprev, s0, prm, l, t_valid):
    b, t, _ = x.shape
    ch = CHUNK
    nc = t // ch
    vec = lambda n: pl.BlockSpec((None, 1, n), lambda bi, c: (l, 0, 0))
    return pl.pallas_call(
        functools.partial(_gdn_body, t_valid=t_valid, padded=t_valid < t),
        grid=(b, nc),
        in_specs=[
            pl.BlockSpec((None, ch, 3 * W_C), lambda bi, c: (bi, c, 0)),
            pl.BlockSpec((None, ch, LANES), lambda bi, c: (bi, c, 0)),
            pl.BlockSpec((None, ch, W_C), lambda bi, c: (bi, c, 0)),
            pl.BlockSpec((None, SUBLANES, 3 * W_C), lambda bi, c: (bi, 0, 0)),
            pl.BlockSpec((None, H_C, D_C, D_C), lambda bi, c: (bi, 0, 0, 0)),
            pl.BlockSpec((None, CONV_W, 3 * W_C), lambda bi, c: (l, 0, 0)),
            vec(LANES), vec(LANES), vec(D_C),
        ],
        out_specs=[
            pl.BlockSpec((None, ch, W_C), lambda bi, c: (bi, c, 0)),
            pl.BlockSpec((None, H_C, D_C, D_C), lambda bi, c: (bi, 0, 0, 0)),
        ],
        out_shape=[jax.ShapeDtypeStruct((b, t, W_C), F32),
                   jax.ShapeDtypeStruct((b, H_C, D_C, D_C), F32)],
        scratch_shapes=[pltpu.VMEM((H_C, D_C, D_C), F32),
                        pltpu.VMEM((SUBLANES + ch, 3 * W_C), F32)],
        compiler_params=_cparams("parallel", "arbitrary"),
        name="gdn",
    )(x, ab, z, cprev, s0, prm["conv_w"], prm["a_log"], prm["dt_bias"], prm["norm_w"])


def _pad_time(x, t_to):
    return jnp.pad(x, ((0, 0), (0, t_to - x.shape[1]), (0, 0)))


def kernel(x_prompt, x_sample, cache_k, cache_v, page_table, state_rwkv, state_rwkv_shift, state_gdn,
           state_gdn_conv, ffn1_norm, ffn1_w_gate, ffn1_w_up, ffn1_w_down, mix_norm, w_in, sb_bias,
           rwkv_mu, rwkv_w0, rwkv_w2, rwkv_a0, rwkv_a2, rwkv_g2, rwkv_k_k, rwkv_k_a, rwkv_r_k,
           rwkv_ln_w, rwkv_ln_b, gdn_conv_w, gdn_a_log, gdn_dt_bias, gdn_norm_w,
           w_br_a, w_br_b, w_br_c, w_out, ffn2_norm, ffn2_w_gate, ffn2_w_up, ffn2_w_down, final_norm):
    bp, t, d = x_prompt.shape
    bs, ts, _ = x_sample.shape
    n_layers = ffn1_norm.shape[0]
    n_pool, _, page, _, _ = cache_k.shape
    assert t % CHUNK == 0 and t % 128 == 0 and ts <= SUBLANES and ts <= CHUNK

    bf = lambda w: w.astype(BF16)
    row3 = lambda w: w.reshape(n_layers, 1, -1)
    f1g, f1u, f1d = bf(ffn1_w_gate), bf(ffn1_w_up), bf(ffn1_w_down)
    f2g, f2u, f2d = bf(ffn2_w_gate), bf(ffn2_w_up), bf(ffn2_w_down)
    o_b = 3 * W_A
    o_c = o_b + R_B
    o_ab = o_c + 3 * W_C
    o_z = o_ab + 2 * H_C
    o_g = o_z + W_C
    w1 = bf(w_in[:, :, :o_c])
    w2 = bf(jnp.concatenate([
        w_in[:, :, o_c:o_ab],
        jnp.pad(w_in[:, :, o_ab:o_z], ((0, 0), (0, 0), (0, LANES - 2 * H_C))),
        w_in[:, :, o_z:], ], axis=-1))
    widths1 = (W_A, W_A, W_A, R_B)
    widths2 = (3 * W_C, LANES, W_C, 3 * d)
    wba, wbb, wbc, wo = bf(w_br_a), bf(w_br_b), bf(w_br_c), bf(w_out)
    zeros_wa = jnp.zeros((n_layers, LORA_W, W_B), F32)
    wwa = bf(jnp.concatenate([
        jnp.concatenate([rwkv_w2, zeros_wa], axis=-1),
        jnp.concatenate([zeros_wa, rwkv_a2], axis=-1)], axis=1))
    gi = jnp.arange(W_B) // D_B
    rwkv_prm = dict(
        mu=row3(rwkv_mu), w0=row3(rwkv_w0), wwa=wwa, a0=row3(rwkv_a0), g2=bf(rwkv_g2),
        k_k=row3(rwkv_k_k), k_a=row3(rwkv_k_a), r_k=row3(rwkv_r_k), ln_w=row3(rwkv_ln_w),
        ln_b=row3(rwkv_ln_b), ones_bd=(gi[:, None] == gi[None, :]).astype(BF16))
    lane_pad = lambda w: jnp.pad(w, ((0, 0), (0, LANES - w.shape[1]))).reshape(n_layers, 1, LANES)
    gdn_prm = dict(conv_w=gdn_conv_w, a_log=lane_pad(gdn_a_log), dt_bias=lane_pad(gdn_dt_bias),
                   norm_w=row3(gdn_norm_w))
    n1a, n1b, n2 = row3(ffn1_norm), row3(mix_norm), row3(ffn2_norm)
    cache_k4 = cache_k.reshape(n_pool, n_layers, page, W_A)
    cache_v4 = cache_v.reshape(n_pool, n_layers, page, W_A)

    xp = x_prompt.reshape(bp * t, d)
    xs = x_sample.reshape(bs * ts, d)
    zero_shift = jnp.zeros((bp, 1, R_B), F32)
    zero_rwkv = jnp.zeros((bp, H_B, D_B, D_B), F32)
    zero_conv = jnp.zeros((bp, SUBLANES, 3 * W_C), F32)
    zero_gdn = jnp.zeros((bp, H_C, D_C, D_C), F32)
    pad8 = lambda y: _pad_time(y.reshape(bs, ts, -1), SUBLANES)
    padc = lambda y: _pad_time(y.reshape(bs, ts, -1), CHUNK)

    outs = [[] for _ in range(12)]
    for l in range(n_layers):
        xp = _ffn(xp, n1a, f1g, f1u, f1d, l, 512)
        xs = _ffn(xs, n1a, f1g, f1u, f1d, l, 512)
        qp, kp, vp, bcp = _proj(xp, n1b, w1, l, widths1, 512)
        cxp, cabp, czp, gtp = _proj(xp, n1b, w2, l, widths2, 256, sigmoid_last=True)
        qs, ks, vs, bcs = _proj(xs, n1b, w1, l, widths1, 512)
        cxs, cabs, czs, gts = _proj(xs, n1b, w2, l, widths2, 256, sigmoid_last=True)

        oap = _attn_prompt(qp, kp, vp, sb_bias[l], bp, t)
        oas = _attn_sample(pad8(qs), pad8(ks), pad8(vs), cache_k4, cache_v4, page_table, sb_bias[l], l)
        oas = oas[:, :ts].reshape(bs * ts, W_A)

        bcp3 = bcp.reshape(bp, t, R_B)
        obp, rwp = _rwkv(bcp3, zero_shift, zero_rwkv, rwkv_prm, l, t)
        bcs3 = bcs.reshape(bs, ts, R_B)
        obs, rws = _rwkv(padc(bcs), state_rwkv_shift[l][:, None, :], state_rwkv[l], rwkv_prm, l, ts)
        obs = obs[:, :ts].reshape(bs * ts, W_B)

        cxp3 = cxp.reshape(bp, t, 3 * W_C)
        ocp, gdp = _gdn(cxp3, cabp.reshape(bp, t, LANES), czp.reshape(bp, t, W_C), zero_conv, zero_gdn,
                        gdn_prm, l, t)
        cxs3 = cxs.reshape(bs, ts, 3 * W_C)
        conv_full = jnp.concatenate([state_gdn_conv[l], cxs3], axis=1)
        cprev_s = jnp.pad(state_gdn_conv[l], ((0, 0), (SUBLANES - (CONV_W - 1), 0), (0, 0)))
        ocs, gds = _gdn(padc(cxs), padc(cabs), padc(czs), cprev_s, state_gdn[l], gdn_prm, l, ts)
        ocs = ocs[:, :ts].reshape(bs * ts, W_C)

        xp = _merge(xp, oap, obp.reshape(bp * t, W_B), ocp.reshape(bp * t, W_C), gtp, wba, wbb, wbc, wo, l, 512)
        xs = _merge(xs, oas, obs, ocs, gts, wba, wbb, wbc, wo, l, 512)
        xp = _ffn(xp, n2, f2g, f2u, f2d, l, 512)
        xs = _ffn(xs, n2, f2g, f2u, f2d, l, 512)

        per_layer = (kp.reshape(bp, t, H_A, D_A), vp.reshape(bp, t, H_A, D_A),
                     ks.reshape(bs, ts, H_A, D_A), vs.reshape(bs, ts, H_A, D_A),
                     rwp, rws, bcp3[:, t - 1], bcs3[:, ts - 1], gdp, gds,
                     cxp3[:, t - (CONV_W - 1):], conv_full[:, ts:])
        for acc, val in zip(outs, per_layer):
            acc.append(val)

    y_prompt = _final_norm(xp, final_norm.reshape(1, d), 512).reshape(bp, t, d)
    y_sample = _final_norm(xs, final_norm.reshape(1, d), 512).reshape(bs, ts, d)
    stack_axes = (1, 1, 1, 1, 0, 0, 0, 0, 0, 0, 0, 0)
    return (y_prompt, y_sample) + tuple(jnp.stack(v, axis=ax) for v, ax in zip(outs, stack_axes))
```

```python
import functools

import jax
import jax.numpy as jnp
from jax import lax
from jax.experimental import pallas as pl
from jax.experimental.pallas import tpu as pltpu

F32 = jnp.float32
BF16 = jnp.bfloat16

NORM_EPS = 1e-6
L2_EPS = 1e-6
H_A, D_A = 8, 64
W_A = H_A * D_A
H_B, D_B = 8, 64
W_B = H_B * D_B
LORA_W, LORA_A, LORA_G = 64, 64, 128
R_B = 3 * W_B + LORA_W + LORA_A + LORA_G
LN_X_EPS = D_B * 1e-5
H_C, D_C = 4, 128
W_C = H_C * D_C
CONV_W = 4
LANES = 128
SUBLANES = 8
HEADS_PER_SYSTEM = 4
SYSTEM_ROWS = 256
PROMPT_CHUNK = SYSTEM_ROWS // HEADS_PER_SYSTEM
TM_DENSE = 512
TM_WIDE_PROJ = 256
TQ_ATTN = 256
VMEM_LIMIT_BYTES = 52 * 1024 * 1024


def _cparams(*sem):
    return pltpu.CompilerParams(dimension_semantics=sem, vmem_limit_bytes=VMEM_LIMIT_BYTES)


def _dot(a, b):
    return jnp.dot(a.astype(BF16), b.astype(BF16), preferred_element_type=F32)


def _dot_nt(a, b):
    return lax.dot_general(a.astype(BF16), b.astype(BF16), (((1,), (1,)), ((), ())),
                           preferred_element_type=F32)


def _dot_tn(a, b):
    return lax.dot_general(a.astype(BF16), b.astype(BF16), (((0,), (0,)), ((), ())),
                           preferred_element_type=F32)


def _split(x):
    hi = x.astype(BF16)
    lo = (x - hi.astype(F32)).astype(BF16)
    return hi, lo


def _dot_exact_rhs(a, b_bf16):
    ah, al = _split(a)
    return (jnp.dot(ah, b_bf16, preferred_element_type=F32)
            + jnp.dot(al, b_bf16, preferred_element_type=F32))


def _dot_exact_lhs(a_bf16, b):
    bh, bl = _split(b)
    return (jnp.dot(a_bf16, bh, preferred_element_type=F32)
            + jnp.dot(a_bf16, bl, preferred_element_type=F32))


def _softplus(z):
    return jnp.maximum(z, 0.0) + jnp.log(1.0 + jnp.exp(-jnp.abs(z)))


def _sigmoid(z):
    return 1.0 / (1.0 + jnp.exp(-z))


def _rms(x, w):
    return x * lax.rsqrt(jnp.mean(x * x, axis=-1, keepdims=True) + NORM_EPS) * w


def _iota2(shape, dim):
    return lax.broadcasted_iota(jnp.int32, shape, dim)


def _tri_inverse(l_mat, n_levels):
    n = l_mat.shape[0]
    eye = jnp.where(_iota2((n, n), 0) == _iota2((n, n), 1), 1.0, 0.0)
    inv = eye + l_mat
    p = l_mat
    for _ in range(n_levels - 1):
        p = _dot(p, p)
        inv = inv + _dot(inv, p)
    return inv


def _time_prefix_matrix(m, ch):
    r = _iota2((m, m), 0)
    c = _iota2((m, m), 1)
    return jnp.where(((r // ch) == (c // ch)) & (c <= r), 1.0, 0.0).astype(BF16)


def _ffn_body(x_ref, nw_ref, wg_ref, wu_ref, wd_ref, o_ref, hn_ref, acc_ref):
    j = pl.program_id(1)

    @pl.when(j == 0)
    def _():
        hn_ref[...] = _rms(x_ref[...], nw_ref[...]).astype(BF16)
        acc_ref[...] = jnp.zeros_like(acc_ref)

    h = hn_ref[...]
    g = jnp.dot(h, wg_ref[...], preferred_element_type=F32)
    u = jnp.dot(h, wu_ref[...], preferred_element_type=F32)
    act = (g * _sigmoid(g) * u).astype(BF16)
    acc_ref[...] += jnp.dot(act, wd_ref[...], preferred_element_type=F32)

    @pl.when(j == pl.num_programs(1) - 1)
    def _():
        o_ref[...] = x_ref[...] + 0.5 * acc_ref[...]


def _ffn(x, nw, wg, wu, wd, l):
    m, d = x.shape
    ff = wg.shape[-1]
    tm = min(TM_DENSE, m)
    tf = ff // 2 if (ff // 2) % LANES == 0 else ff
    return pl.pallas_call(
        _ffn_body,
        grid=(m // tm, ff // tf),
        in_specs=[
            pl.BlockSpec((tm, d), lambda i, j: (i, 0)),
            pl.BlockSpec((None, 1, d), lambda i, j: (l, 0, 0)),
            pl.BlockSpec((None, d, tf), lambda i, j: (l, 0, j)),
            pl.BlockSpec((None, d, tf), lambda i, j: (l, 0, j)),
            pl.BlockSpec((None, tf, d), lambda i, j: (l, j, 0)),
        ],
        out_specs=pl.BlockSpec((tm, d), lambda i, j: (i, 0)),
        out_shape=jax.ShapeDtypeStruct((m, d), F32),
        scratch_shapes=[pltpu.VMEM((tm, d), BF16), pltpu.VMEM((tm, d), F32)],
        compiler_params=_cparams("parallel", "arbitrary"),
        name="ffn",
    )(x, nw, wg, wu, wd)


def _proj_body(x_ref, nw_ref, w_ref, *o_refs, widths, sigmoid_last):
    h = _rms(x_ref[...], nw_ref[...]).astype(BF16)
    off = 0
    for idx, (o_ref, wd) in enumerate(zip(o_refs, widths)):
        y = jnp.dot(h, w_ref[:, off:off + wd], preferred_element_type=F32)
        if sigmoid_last and idx == len(widths) - 1:
            y = _sigmoid(y)
        o_ref[...] = y
        off += wd


def _proj(x, nw, w, l, widths, tm, sigmoid_last=False):
    m, d = x.shape
    n = w.shape[-1]
    assert sum(widths) == n
    tm = min(tm, m)
    return pl.pallas_call(
        functools.partial(_proj_body, widths=widths, sigmoid_last=sigmoid_last),
        grid=(m // tm,),
        in_specs=[
            pl.BlockSpec((tm, d), lambda i: (i, 0)),
            pl.BlockSpec((None, 1, d), lambda i: (l, 0, 0)),
            pl.BlockSpec((None, d, n), lambda i: (l, 0, 0)),
        ],
        out_specs=[pl.BlockSpec((tm, wd), lambda i: (i, 0)) for wd in widths],
        out_shape=[jax.ShapeDtypeStruct((m, wd), F32) for wd in widths],
        compiler_params=_cparams("parallel"),
        name="proj",
    )(x, nw, w)


def _merge_body(x_ref, oa_ref, ob_ref, oc_ref, ga_ref, gb_ref, gc_ref,
                wa_ref, wb_ref, wc_ref, wo_ref, o_ref):
    merged = (ga_ref[...] * _dot(oa_ref[...], wa_ref[...])
              + gb_ref[...] * _dot(ob_ref[...], wb_ref[...])
              + gc_ref[...] * _dot(oc_ref[...], wc_ref[...]))
    o_ref[...] = x_ref[...] + _dot(merged, wo_ref[...])


def _merge(x, oa, ob, oc, gates, wa, wb, wc, wo, l):
    m, d = x.shape
    tm = min(TM_DENSE, m)
    row = lambda i: (i, 0)
    wspec = lambda k: pl.BlockSpec((None, k, d), lambda i: (l, 0, 0))
    return pl.pallas_call(
        _merge_body,
        grid=(m // tm,),
        in_specs=[
            pl.BlockSpec((tm, d), row),
            pl.BlockSpec((tm, W_A), row), pl.BlockSpec((tm, W_B), row), pl.BlockSpec((tm, W_C), row),
            pl.BlockSpec((tm, d), lambda i: (i, 0)),
            pl.BlockSpec((tm, d), lambda i: (i, 1)),
            pl.BlockSpec((tm, d), lambda i: (i, 2)),
            wspec(W_A), wspec(W_B), wspec(W_C), wspec(d),
        ],
        out_specs=pl.BlockSpec((tm, d), row),
        out_shape=jax.ShapeDtypeStruct((m, d), F32),
        compiler_params=_cparams("parallel"),
        name="merge",
    )(x, oa, ob, oc, gates, gates, gates, wa, wb, wc, wo)


def _norm_body(x_ref, w_ref, o_ref):
    o_ref[...] = _rms(x_ref[...], w_ref[...])


def _final_norm(x, w):
    m, d = x.shape
    tm = min(TM_DENSE, m)
    return pl.pallas_call(
        _norm_body,
        grid=(m // tm,),
        in_specs=[pl.BlockSpec((tm, d), lambda i: (i, 0)), pl.BlockSpec((1, d), lambda i: (0, 0))],
        out_specs=pl.BlockSpec((tm, d), lambda i: (i, 0)),
        out_shape=jax.ShapeDtypeStruct((m, d), F32),
        compiler_params=_cparams("parallel"),
        name="final_norm",
    )(x, w)


def _sb_sweep(chains, tri):
    sps = []
    for zs, masks, _, _, _ in chains:
        row = []
        for z, mask in zip(zs, masks):
            sp = _softplus(z)
            if mask is not None:
                sp = jnp.where(mask, sp, 0.0)
            row.append(sp.astype(BF16))
        sps.append(row)
    css = [[jnp.dot(sp, tri, preferred_element_type=F32) for sp in row] for row in sps]
    state = [[o, r] for _, _, _, o, r in chains]
    for i in range(max(len(c[0]) for c in chains)):
        for st, (zs, masks, pvs, _, _), row in zip(state, chains, css):
            if i < len(zs):
                a = jnp.exp(zs[i] - (st[1] + row[i]))
                if masks[i] is not None:
                    a = jnp.where(masks[i], a, 0.0)
                st[0] = st[0] + pvs[i](a.astype(BF16))
                st[1] = st[1] + row[i][:, 0:1]
    return tuple((o, r) for o, r in state)


def _suffix_matrix(tk):
    return jnp.where(_iota2((tk, tk), 0) >= _iota2((tk, tk), 1), 1.0, 0.0).astype(BF16)


def _attn_prompt_body(bias_ref, q_ref, k_ref, v_ref, o_ref, kb_ref, vb_ref, *, tq):
    p = pl.program_id(1)
    qi = pl.program_id(2)
    heads = LANES // D_A

    @pl.when(qi == 0)
    def _():
        kb_ref[...] = k_ref[...].astype(BF16)
        vb_ref[...] = v_ref[...].astype(BF16)

    q = q_ref[...] * (D_A ** -0.5)
    lane = _iota2((tq, LANES), 1)
    causal = _iota2((tq, tq), 1) < _iota2((tq, tq), 0)
    tri = _suffix_matrix(tq)
    qhs = [jnp.where(lane // D_A == hh, q, 0.0).astype(BF16) for hh in range(heads)]
    biases = [bias_ref[heads * p + hh] for hh in range(heads)]

    def blocks(js, masks, carry):
        kv = []
        for j in js:
            start = pl.multiple_of(j * tq, tq)
            kv.append((kb_ref[pl.ds(start, tq), :], vb_ref[pl.ds(start, tq), :]))
        pvs = [lambda a, vblk=vblk: jnp.dot(a, vblk, preferred_element_type=F32) for _, vblk in kv]
        chains = []
        for hh in range(heads):
            zs = [_dot_nt(qhs[hh], kblk) + biases[hh] for kblk, _ in kv]
            chains.append((zs, masks, pvs) + tuple(carry[hh]))
        return _sb_sweep(chains, tri)

    init = tuple((jnp.zeros((tq, LANES), F32), jnp.zeros((tq, 1), F32)) for _ in range(heads))
    carry = blocks([qi], [causal], init)

    def pair(jj, c):
        j = qi - 1 - 2 * jj
        return blocks([j, j - 1], [None, None], c)

    carry = lax.fori_loop(0, qi // 2, pair, carry)
    carry = lax.cond(qi % 2 == 1, lambda c: blocks([0], [None], c), lambda c: c, carry)
    o_ref[...] = jnp.where(lane < D_A, carry[0][0], carry[1][0])


def _attn_prompt(q, k, v, bias, b, t):
    m = b * t
    tq = min(TQ_ATTN, t)
    nq = t // tq
    npair = W_A // LANES
    grid_spec = pltpu.PrefetchScalarGridSpec(
        num_scalar_prefetch=1,
        grid=(b, npair, nq),
        in_specs=[
            pl.BlockSpec((tq, LANES), lambda bi, p, qi, bias: (bi * nq + qi, p)),
            pl.BlockSpec((t, LANES), lambda bi, p, qi, bias: (bi, p)),
            pl.BlockSpec((t, LANES), lambda bi, p, qi, bias: (bi, p)),
        ],
        out_specs=pl.BlockSpec((tq, LANES), lambda bi, p, qi, bias: (bi * nq + qi, p)),
        scratch_shapes=[pltpu.VMEM((t, LANES), BF16), pltpu.VMEM((t, LANES), BF16)],
    )
    return pl.pallas_call(
        functools.partial(_attn_prompt_body, tq=tq),
        grid_spec=grid_spec,
        out_shape=jax.ShapeDtypeStruct((m, W_A), F32),
        compiler_params=_cparams("parallel", "parallel", "arbitrary"),
        name="attn_prompt",
    )(bias, q, k, v)


def _attn_sample_body(pt_ref, bias_ref, q_ref, kn_ref, vn_ref, *rest, n_pages, page):
    kp_refs = rest[:n_pages]
    vp_refs = rest[n_pages:2 * n_pages]
    o_ref = rest[2 * n_pages]
    nrow = H_A * SUBLANES
    q8 = q_ref[...] * (D_A ** -0.5)
    head_of_lane = _iota2((SUBLANES, W_A), 1) // D_A
    qm = jnp.concatenate([jnp.where(head_of_lane == h, q8, 0.0) for h in range(H_A)], axis=0).astype(BF16)
    ri = _iota2((nrow, page), 0)
    bias = jnp.zeros((nrow, page), F32)
    for h in range(H_A):
        bias = jnp.where(ri // SUBLANES == h, bias_ref[h], bias)
    mask_new = _iota2((nrow, page), 1) < (ri % SUBLANES)
    tri = _suffix_matrix(page)
    pad = jnp.zeros((page - SUBLANES, W_A), F32)
    k_new = jnp.concatenate([kn_ref[...], pad], axis=0).astype(BF16)
    v_new = jnp.concatenate([vn_ref[...], pad], axis=0).astype(BF16)
    zs = [_dot_nt(qm, k_new) + bias]
    masks = [mask_new]
    pvs = [lambda a: jnp.dot(a, v_new, preferred_element_type=F32)]
    for j in reversed(range(n_pages)):
        zs.append(jnp.dot(qm, kp_refs[j][...].astype(BF16), preferred_element_type=F32) + bias)
        masks.append(None)
        pvs.append(lambda a, j=j: _dot_nt(a, vp_refs[j][...].astype(BF16)))
    chain = (zs, masks, pvs, jnp.zeros((nrow, W_A), F32), jnp.zeros((nrow, 1), F32))
    ((o, _),) = _sb_sweep([chain], tri)
    out = jnp.zeros((SUBLANES, W_A), F32)
    for h in range(H_A):
        out = jnp.where(head_of_lane == h, o[h * SUBLANES:(h + 1) * SUBLANES, :], out)
    o_ref[...] = out


def _attn_sample(q8, k8, v8, cache_kt, cache_vt, page_table, bias, l):
    bs = q8.shape[0]
    n_pages = page_table.shape[1]
    page = cache_kt.shape[3]
    new_spec = pl.BlockSpec((None, SUBLANES, W_A), lambda bi, pt, bias: (bi, 0, 0))

    def page_spec(j):
        return pl.BlockSpec((None, None, W_A, page), lambda bi, pt, bias, j=j: (pt[bi, j], l, 0, 0))

    grid_spec = pltpu.PrefetchScalarGridSpec(
        num_scalar_prefetch=2,
        grid=(bs,),
        in_specs=[new_spec, new_spec, new_spec]
        + [page_spec(j) for j in range(n_pages)] + [page_spec(j) for j in range(n_pages)],
        out_specs=new_spec,
    )
    return pl.pallas_call(
        functools.partial(_attn_sample_body, n_pages=n_pages, page=page),
        grid_spec=grid_spec,
        out_shape=jax.ShapeDtypeStruct((bs, SUBLANES, W_A), F32),
        compiler_params=_cparams("parallel"),
        name="attn_sample",
    )(page_table, bias, q8, k8, v8, *([cache_kt] * n_pages), *([cache_vt] * n_pages))


def _group_sum(x, ones_bd):
    nt = x.shape[1] // LANES
    stacked = jnp.concatenate([x[:, i * LANES:(i + 1) * LANES] for i in range(nt)], axis=0)
    s = _dot_exact_rhs(stacked, ones_bd)
    m = x.shape[0]
    return jnp.concatenate([s[i * m:(i + 1) * m] for i in range(nt)], axis=1)


def _rwkv_body(cols_ref, shift_ref, s0_ref, mu_ref, w0_ref, wwa_ref, a0_ref, g2_ref, kk_ref, ka_ref,
               rk_ref, lnw_ref, lnb_ref, ones_ref, ob_ref, sout_ref, s_ref, prev_ref, *, t_valid, padded):
    c = pl.program_id(1)
    nc = pl.num_programs(1)
    nb, ch, _ = cols_ref.shape
    m = nb * ch
    hps = HEADS_PER_SYSTEM
    grp = hps * D_B
    n_grp = W_B // grp
    sub = hps * ch

    @pl.when(c == 0)
    def _():
        prev_ref[...] = shift_ref[...]
        s_ref[...] = jnp.zeros_like(s_ref)
        for e in range(nb):
            for h in range(H_B):
                g, hh = divmod(h, hps)
                s_ref[e, g, hh * D_B:(hh + 1) * D_B, hh * D_B:(hh + 1) * D_B] = s0_ref[e, h]

    cols3 = cols_ref[...]
    cols = cols3.reshape(m, R_B)
    row = _iota2((m, 1), 0)
    first = jnp.broadcast_to(prev_ref[...], (nb, ch, R_B)).reshape(m, R_B)
    prev = jnp.where(row % ch == 0, first, pltpu.roll(cols, 1, 0))
    prev_ref[...] = cols3[:, ch - 1:ch, :]
    mixed = cols + mu_ref[...] * (prev - cols)
    r = mixed[:, 0:W_B]
    k = mixed[:, W_B:2 * W_B]
    v = mixed[:, 2 * W_B:3 * W_B]
    lora = mixed[:, 3 * W_B:3 * W_B + LORA_W + LORA_A]
    gl = mixed[:, 3 * W_B + LORA_W + LORA_A:]
    lora = jnp.where(_iota2(lora.shape, 1) < LORA_W, jnp.tanh(lora), lora)
    wa = _dot(lora, wwa_ref[...])
    w_log = -_softplus(-(w0_ref[...] + wa[:, :W_B])) - 0.5
    logw = -jnp.exp(w_log)
    a = _sigmoid(a0_ref[...] + wa[:, W_B:])
    g = _dot(_sigmoid(gl), g2_ref[...])
    ones_bd = ones_ref[...]
    kk = k * kk_ref[...]
    k2 = k * (1.0 + (a - 1.0) * ka_ref[...])
    sums = _group_sum(jnp.concatenate([kk * kk, r * k2 * rk_ref[...]], axis=0), ones_bd)
    kk = kk * lax.rsqrt(sums[0:m] + L2_EPS)
    bonus = sums[m:2 * m] * v
    v_s = v
    if padded:
        valid = (row % ch) + c * ch < t_valid
        logw = jnp.where(valid, logw, 0.0)
        kk = jnp.where(valid, kk, 0.0)
        k2 = jnp.where(valid, k2, 0.0)
        v_s = jnp.where(valid, v, 0.0)
    a_s = -kk
    b_s = kk * a

    cum = _dot_exact_lhs(_time_prefix_matrix(m, ch), logw)
    e_in = jnp.exp(cum)
    e_neg = jnp.exp(-cum)
    at = a_s * jnp.exp(cum - logw)
    bt = b_s * e_neg
    kt = k2 * e_neg
    rt = r * e_in

    n = nb * sub
    bm = ((_iota2((n, grp), 0) // ch) % hps) == (_iota2((n, grp), 1) // D_B)
    pr = _iota2((n, n), 0)
    pc = _iota2((n, n), 1)
    same = (pr // ch) == (pc // ch)
    strict = same & (pc < pr)
    incl = same & (pc <= pr)
    n_levels = max(1, (ch - 1).bit_length())

    groups = range(n_grp)
    lanes = [slice(gi * grp, (gi + 1) * grp) for gi in groups]
    per_seq = lambda x: [x[e * sub:(e + 1) * sub] for e in range(nb)]
    cat = lambda xs: xs[0] if len(xs) == 1 else jnp.concatenate(xs, axis=0)

    def bd(x, sl):
        pieces = [x[e * ch:(e + 1) * ch, sl] for e in range(nb) for _ in range(hps)]
        return jnp.where(bm, jnp.concatenate(pieces, axis=0), 0.0).astype(BF16)

    ab, bb, kb, rb, vb = ([bd(x, sl) for sl in lanes] for x in (at, bt, kt, rt, v_s))
    ar = [jnp.concatenate([a_, r_], axis=0) for a_, r_ in zip(ab, rb)]
    s_old = [[s_ref[e, gi] for e in range(nb)] for gi in groups]
    s16 = [[s.astype(BF16) for s in ss] for ss in s_old]
    ar_b = [_dot_nt(x, y_) for x, y_ in zip(ar, bb)]
    ar_k = [_dot_nt(x, y_) for x, y_ in zip(ar, kb)]
    l_ab = [jnp.where(strict, x[0:n], 0.0) for x in ar_b]
    m_rb = [jnp.where(incl, x[n:2 * n], 0.0).astype(BF16) for x in ar_b]
    l_ak = [jnp.where(strict, x[0:n], 0.0) for x in ar_k]
    m_rk = [jnp.where(incl, x[n:2 * n], 0.0) for x in ar_k]
    a_s0 = [cat([_dot_nt(x, s) for x, s in zip(per_seq(a_), ss)]) for a_, ss in zip(ab, s16)]
    r_s0 = [cat([_dot_nt(x, s) for x, s in zip(per_seq(r_), ss)]) for r_, ss in zip(rb, s16)]
    rhs = [x + _dot(l, v_) for x, l, v_ in zip(a_s0, l_ak, vb)]
    y_v = [x + _dot(mk, v_) for x, mk, v_ in zip(r_s0, m_rk, vb)]
    eye = jnp.where(pr == pc, 1.0, 0.0)
    inv = [eye + l for l in l_ab]
    pw = l_ab
    for _ in range(n_levels - 1):
        pw = [_dot(p_, p_) for p_ in pw]
        inv = [i_ + _dot(i_, p_) for i_, p_ in zip(inv, pw)]
    ub = [_dot(i_, x).astype(BF16) for i_, x in zip(inv, rhs)]
    ys = []
    for gi in groups:
        y = y_v[gi] + jnp.dot(m_rb[gi], ub[gi], preferred_element_type=F32)
        ys.append(cat([sum(ye[h * ch:(h + 1) * ch] for h in range(hps)) for ye in per_seq(y)]))
        seqs = zip(per_seq(ub[gi]), per_seq(bb[gi]), per_seq(vb[gi]), per_seq(kb[gi]))
        for e, (ue, be, ve, ke) in enumerate(seqs):
            wc = e_in[e * ch + ch - 1:e * ch + ch, lanes[gi]]
            s_ref[e, gi] = (s_old[gi][e] + _dot_tn(ue, be) + _dot_tn(ve, ke)) * wc
    y = jnp.concatenate(ys, axis=1)

    inv_d = 1.0 / D_B
    dev = y - _group_sum(y, ones_bd) * inv_d
    var = _group_sum(dev * dev, ones_bd) * inv_d
    yn = dev * lax.rsqrt(var + LN_X_EPS) * lnw_ref[...] + lnb_ref[...]
    ob_ref[...] = ((yn + bonus) * g).reshape(nb, ch, W_B)

    @pl.when(c == nc - 1)
    def _():
        for e in range(nb):
            for h in range(H_B):
                gq, hh = divmod(h, hps)
                sout_ref[e, h] = s_ref[e, gq, hh * D_B:(hh + 1) * D_B, hh * D_B:(hh + 1) * D_B]


def _rwkv(cols, shift, s0, prm, l, t_valid, ch):
    b, t, _ = cols.shape
    nb = SYSTEM_ROWS // (HEADS_PER_SYSTEM * ch)
    nc = t // ch
    assert b % nb == 0 and t % ch == 0 and (nb == 1 or nc == 1)
    vec = lambda n: pl.BlockSpec((None, 1, n), lambda bi, c: (l, 0, 0))
    mat = lambda k, n: pl.BlockSpec((None, k, n), lambda bi, c: (l, 0, 0))
    return pl.pallas_call(
        functools.partial(_rwkv_body, t_valid=t_valid, padded=t_valid < t),
        grid=(b // nb, nc),
        in_specs=[
            pl.BlockSpec((nb, ch, R_B), lambda bi, c: (bi, c, 0)),
            pl.BlockSpec((nb, 1, R_B), lambda bi, c: (bi, 0, 0)),
            pl.BlockSpec((nb, H_B, D_B, D_B), lambda bi, c: (bi, 0, 0, 0)),
            vec(R_B), vec(W_B), mat(LORA_W + LORA_A, 2 * W_B), vec(W_B), mat(LORA_G, W_B),
            vec(W_B), vec(W_B), vec(W_B), vec(W_B), vec(W_B),
            pl.BlockSpec((LANES, LANES), lambda bi, c: (0, 0)),
        ],
        out_specs=[
            pl.BlockSpec((nb, ch, W_B), lambda bi, c: (bi, c, 0)),
            pl.BlockSpec((nb, H_B, D_B, D_B), lambda bi, c: (bi, 0, 0, 0)),
        ],
        out_shape=[jax.ShapeDtypeStruct((b, t, W_B), F32),
                   jax.ShapeDtypeStruct((b, H_B, D_B, D_B), F32)],
        scratch_shapes=[pltpu.VMEM((nb, W_B // (HEADS_PER_SYSTEM * D_B), HEADS_PER_SYSTEM * D_B,
                                    HEADS_PER_SYSTEM * D_B), F32),
                        pltpu.VMEM((nb, 1, R_B), F32)],
        compiler_params=_cparams("parallel", "arbitrary"),
        name="rwkv7",
    )(cols, shift, s0, prm["mu"], prm["w0"], prm["wwa"], prm["a0"], prm["g2"], prm["k_k"], prm["k_a"],
      prm["r_k"], prm["ln_w"], prm["ln_b"], prm["ones_bd"])


def _gdn_body(x_ref, ab_ref, z_ref, cprev_ref, s0_ref, cw_ref, alog_ref, dtb_ref, nw_ref,
              oc_ref, sout_ref, s_ref, xbuf_ref, *, t_valid, padded):
    c = pl.program_id(1)
    nc = pl.num_programs(1)
    nb, ch, _ = x_ref.shape
    m = nb * ch
    hist = SUBLANES

    @pl.when(c == 0)
    def _():
        xbuf_ref[:, 0:hist, :] = cprev_ref[...]
        s_ref[...] = s0_ref[...]

    x3 = x_ref[...]
    xbuf_ref[:, hist:hist + ch, :] = x3
    cw = cw_ref[...]
    conv = x3.reshape(m, 3 * W_C) * cw[CONV_W - 1:CONV_W, :]
    for i in range(1, CONV_W):
        conv = conv + xbuf_ref[:, hist - i:hist - i + ch, :].reshape(m, 3 * W_C) * cw[CONV_W - 1 - i:CONV_W - i, :]
    conv = conv * _sigmoid(conv)
    xbuf_ref[:, 0:hist, :] = x3[:, ch - hist:ch, :]

    def stack(y):
        return jnp.concatenate([y[e * ch:(e + 1) * ch, h * D_C:(h + 1) * D_C]
                                for e in range(nb) for h in range(H_C)], axis=0)

    q = stack(conv[:, 0:W_C])
    k = stack(conv[:, W_C:2 * W_C])
    v = stack(conv[:, 2 * W_C:3 * W_C])
    q = q * lax.rsqrt(jnp.sum(q * q, axis=-1, keepdims=True) + L2_EPS) * (D_C ** -0.5)
    k = k * lax.rsqrt(jnp.sum(k * k, axis=-1, keepdims=True) + L2_EPS)

    ab = ab_ref[...].reshape(m, LANES)
    g_all = -jnp.exp(alog_ref[...]) * _softplus(ab + dtb_ref[...])
    beta_all = _sigmoid(ab)
    if padded:
        valid = (_iota2((m, 1), 0) % ch) + c * ch < t_valid
        g_all = jnp.where(valid, g_all, 0.0)
        beta_all = jnp.where(valid, beta_all, 0.0)
    cum_all = _dot_exact_lhs(_time_prefix_matrix(m, ch), g_all)

    def rows(src, lane0, last_only=False):
        out = []
        for e in range(nb):
            for h in range(H_C):
                lo = e * ch + (ch - 1 if last_only else 0)
                piece = src[lo:(e + 1) * ch, lane0 + h:lane0 + h + 1]
                out.append(jnp.broadcast_to(piece, (ch, D_C)))
        return jnp.concatenate(out, axis=0)

    cum = rows(cum_all, 0)
    beta = rows(beta_all, H_C)
    g_last = rows(cum_all, 0, last_only=True)

    n = nb * H_C * ch
    pr = _iota2((n, n), 0)
    pc = _iota2((n, n), 1)
    same = (pr // ch) == (pc // ch)
    strict = same & (pc < pr)
    incl = same & (pc <= pr)
    cum_col = jnp.concatenate([cum] * (n // D_C), axis=1)
    cum_row = jnp.broadcast_to(jnp.transpose(cum)[0:1, :], (n, n))
    decay = jnp.where(incl, jnp.exp(jnp.where(incl, cum_col - cum_row, 0.0)), 0.0)

    kbeta = k * beta
    kb16 = k.astype(BF16)
    a_mat = jnp.where(strict, _dot_nt(kbeta, kb16) * decay, 0.0)
    t_inv = _tri_inverse(-a_mat, max(1, (ch - 1).bit_length()))
    e_cum = jnp.exp(cum)
    uw = _dot(t_inv, jnp.concatenate([v * beta, kbeta * e_cum], axis=1))
    u_cap = uw[:, :D_C]
    w_cap = uw[:, D_C:]
    qk = _dot_nt(q, kb16) * decay
    q_dec = q * e_cum
    k_dec = k * jnp.exp(g_last - cum)

    v_new, o_state = [], []
    for e in range(nb):
        for h in range(H_C):
            i = e * H_C + h
            sl = slice(i * ch, (i + 1) * ch)
            s_old = s_ref[e, h]
            ws = _dot(jnp.concatenate([w_cap[sl], q_dec[sl]], axis=0), s_old)
            vn = u_cap[sl] - ws[0:ch]
            o_state.append(ws[ch:2 * ch])
            v_new.append(vn)
            s_ref[e, h] = s_old * jnp.exp(g_last[i * ch:i * ch + 1, :]) + _dot_tn(k_dec[sl], vn)
    o = jnp.concatenate(o_state, axis=0) + _dot(qk, jnp.concatenate(v_new, axis=0))

    z3 = z_ref[...]
    for e in range(nb):
        for h in range(H_C):
            i = e * H_C + h
            oh = o[i * ch:(i + 1) * ch]
            oh = oh * lax.rsqrt(jnp.mean(oh * oh, axis=-1, keepdims=True) + NORM_EPS) * nw_ref[...]
            zh = z3[e, :, h * D_C:(h + 1) * D_C]
            oc_ref[e, :, h * D_C:(h + 1) * D_C] = oh * (zh * _sigmoid(zh))

    @pl.when(c == nc - 1)
    def _():
        sout_ref[...] = s_ref[...]


def _gdn(x, ab, z, cprev, s0, prm, l, t_valid, ch):
    b, t, _ = x.shape
    nb = SYSTEM_ROWS // (H_C * ch)
    nc = t // ch
    assert b % nb == 0 and t % ch == 0 and (nb == 1 or nc == 1) and ch >= SUBLANES
    vec = lambda n: pl.BlockSpec((None, 1, n), lambda bi, c: (l, 0, 0))
    return pl.pallas_call(
        functools.partial(_gdn_body, t_valid=t_valid, padded=t_valid < t),
        grid=(b // nb, nc),
        in_specs=[
            pl.BlockSpec((nb, ch, 3 * W_C), lambda bi, c: (bi, c, 0)),
            pl.BlockSpec((nb, ch, LANES), lambda bi, c: (bi, c, 0)),
            pl.BlockSpec((nb, ch, W_C), lambda bi, c: (bi, c, 0)),
            pl.BlockSpec((nb, SUBLANES, 3 * W_C), lambda bi, c: (bi, 0, 0)),
            pl.BlockSpec((nb, H_C, D_C, D_C), lambda bi, c: (bi, 0, 0, 0)),
            pl.BlockSpec((None, CONV_W, 3 * W_C), lambda bi, c: (l, 0, 0)),
            vec(LANES), vec(LANES), vec(D_C),
        ],
        out_specs=[
            pl.BlockSpec((nb, ch, W_C), lambda bi, c: (bi, c, 0)),
            pl.BlockSpec((nb, H_C, D_C, D_C), lambda bi, c: (bi, 0, 0, 0)),
        ],
        out_shape=[jax.ShapeDtypeStruct((b, t, W_C), F32),
                   jax.ShapeDtypeStruct((b, H_C, D_C, D_C), F32)],
        scratch_shapes=[pltpu.VMEM((nb, H_C, D_C, D_C), F32),
                        pltpu.VMEM((nb, SUBLANES + ch, 3 * W_C), F32)],
        compiler_params=_cparams("parallel", "arbitrary"),
        name="gdn",
    )(x, ab, z, cprev, s0, prm["conv_w"], prm["a_log"], prm["dt_bias"], prm["norm_w"])


def _pad_time(x, t_to):
    return jnp.pad(x, ((0, 0), (0, t_to - x.shape[1]), (0, 0)))


def kernel(x_prompt, x_sample, cache_k, cache_v, page_table, state_rwkv, state_rwkv_shift, state_gdn,
           state_gdn_conv, ffn1_norm, ffn1_w_gate, ffn1_w_up, ffn1_w_down, mix_norm, w_in, sb_bias,
           rwkv_mu, rwkv_w0, rwkv_w2, rwkv_a0, rwkv_a2, rwkv_g2, rwkv_k_k, rwkv_k_a, rwkv_r_k,
           rwkv_ln_w, rwkv_ln_b, gdn_conv_w, gdn_a_log, gdn_dt_bias, gdn_norm_w,
           w_br_a, w_br_b, w_br_c, w_out, ffn2_norm, ffn2_w_gate, ffn2_w_up, ffn2_w_down, final_norm):
    bp, t, d = x_prompt.shape
    bs, ts, _ = x_sample.shape
    n_layers = ffn1_norm.shape[0]
    n_pool, _, page, _, _ = cache_k.shape
    assert t % PROMPT_CHUNK == 0 and t % TQ_ATTN == 0 and ts <= SUBLANES

    bf = lambda w: w.astype(BF16)
    row3 = lambda w: w.reshape(n_layers, 1, -1)
    f1g, f1u, f1d = bf(ffn1_w_gate), bf(ffn1_w_up), bf(ffn1_w_down)
    f2g, f2u, f2d = bf(ffn2_w_gate), bf(ffn2_w_up), bf(ffn2_w_down)
    o_b = 3 * W_A
    o_c = o_b + R_B
    o_ab = o_c + 3 * W_C
    o_z = o_ab + 2 * H_C
    w1 = bf(w_in[:, :, :o_c])
    w2 = bf(jnp.concatenate([
        w_in[:, :, o_c:o_ab],
        jnp.pad(w_in[:, :, o_ab:o_z], ((0, 0), (0, 0), (0, LANES - 2 * H_C))),
        w_in[:, :, o_z:], ], axis=-1))
    widths1 = (W_A, W_A, W_A, R_B)
    widths2 = (3 * W_C, LANES, W_C, 3 * d)
    wba, wbb, wbc, wo = bf(w_br_a), bf(w_br_b), bf(w_br_c), bf(w_out)
    zeros_wa = jnp.zeros((n_layers, LORA_W, W_B), F32)
    wwa = bf(jnp.concatenate([
        jnp.concatenate([rwkv_w2, zeros_wa], axis=-1),
        jnp.concatenate([zeros_wa, rwkv_a2], axis=-1)], axis=1))
    gi = jnp.arange(LANES) // D_B
    rwkv_prm = dict(
        mu=row3(rwkv_mu), w0=row3(rwkv_w0), wwa=wwa, a0=row3(rwkv_a0), g2=bf(rwkv_g2),
        k_k=row3(rwkv_k_k), k_a=row3(rwkv_k_a), r_k=row3(rwkv_r_k), ln_w=row3(rwkv_ln_w),
        ln_b=row3(rwkv_ln_b), ones_bd=(gi[:, None] == gi[None, :]).astype(BF16))
    lane_pad = lambda w: jnp.pad(w, ((0, 0), (0, LANES - w.shape[1]))).reshape(n_layers, 1, LANES)
    gdn_prm = dict(conv_w=gdn_conv_w, a_log=lane_pad(gdn_a_log), dt_bias=lane_pad(gdn_dt_bias),
                   norm_w=row3(gdn_norm_w))
    n1a, n1b, n2 = row3(ffn1_norm), row3(mix_norm), row3(ffn2_norm)
    cache_kt = jnp.transpose(cache_k, (0, 1, 3, 4, 2)).reshape(n_pool, n_layers, W_A, page)
    cache_vt = jnp.transpose(cache_v, (0, 1, 3, 4, 2)).reshape(n_pool, n_layers, W_A, page)

    xp = x_prompt.reshape(bp * t, d)
    xs = x_sample.reshape(bs * ts, d)
    zero_shift = jnp.zeros((bp, 1, R_B), F32)
    zero_rwkv = jnp.zeros((bp, H_B, D_B, D_B), F32)
    zero_conv = jnp.zeros((bp, SUBLANES, 3 * W_C), F32)
    zero_gdn = jnp.zeros((bp, H_C, D_C, D_C), F32)
    pad8 = lambda y: _pad_time(y.reshape(bs, ts, -1), SUBLANES)

    outs = [[] for _ in range(12)]
    for l in range(n_layers):
        xp = _ffn(xp, n1a, f1g, f1u, f1d, l)
        xs = _ffn(xs, n1a, f1g, f1u, f1d, l)
        qp, kp, vp, bcp = _proj(xp, n1b, w1, l, widths1, TM_DENSE)
        cxp, cabp, czp, gtp = _proj(xp, n1b, w2, l, widths2, TM_WIDE_PROJ, sigmoid_last=True)
        qs, ks, vs, bcs = _proj(xs, n1b, w1, l, widths1, TM_DENSE)
        cxs, cabs, czs, gts = _proj(xs, n1b, w2, l, widths2, TM_WIDE_PROJ, sigmoid_last=True)

        oap = _attn_prompt(qp, kp, vp, sb_bias[l], bp, t)
        oas = _attn_sample(pad8(qs), pad8(ks), pad8(vs), cache_kt, cache_vt, page_table, sb_bias[l], l)
        oas = oas[:, :ts].reshape(bs * ts, W_A)

        bcp3 = bcp.reshape(bp, t, R_B)
        obp, rwp = _rwkv(bcp3, zero_shift, zero_rwkv, rwkv_prm, l, t, PROMPT_CHUNK)
        bcs3 = bcs.reshape(bs, ts, R_B)
        obs, rws = _rwkv(pad8(bcs), state_rwkv_shift[l][:, None, :], state_rwkv[l], rwkv_prm, l, ts, SUBLANES)
        obs = obs[:, :ts].reshape(bs * ts, W_B)

        cxp3 = cxp.reshape(bp, t, 3 * W_C)
        ocp, gdp = _gdn(cxp3, cabp.reshape(bp, t, LANES), czp.reshape(bp, t, W_C), zero_conv, zero_gdn,
                        gdn_prm, l, t, PROMPT_CHUNK)
        cxs3 = cxs.reshape(bs, ts, 3 * W_C)
        conv_full = jnp.concatenate([state_gdn_conv[l], cxs3], axis=1)
        cprev_s = jnp.pad(state_gdn_conv[l], ((0, 0), (SUBLANES - (CONV_W - 1), 0), (0, 0)))
        ocs, gds = _gdn(pad8(cxs), pad8(cabs), pad8(czs), cprev_s, state_gdn[l], gdn_prm, l, ts, SUBLANES)
        ocs = ocs[:, :ts].reshape(bs * ts, W_C)

        xp = _merge(xp, oap, obp.reshape(bp * t, W_B), ocp.reshape(bp * t, W_C), gtp, wba, wbb, wbc, wo, l)
        xs = _merge(xs, oas, obs, ocs, gts, wba, wbb, wbc, wo, l)
        xp = _ffn(xp, n2, f2g, f2u, f2d, l)
        xs = _ffn(xs, n2, f2g, f2u, f2d, l)

        per_layer = (kp.reshape(bp, t, H_A, D_A), vp.reshape(bp, t, H_A, D_A),
                     ks.reshape(bs, ts, H_A, D_A), vs.reshape(bs, ts, H_A, D_A),
                     rwp, rws, bcp3[:, t - 1], bcs3[:, ts - 1], gdp, gds,
                     cxp3[:, t - (CONV_W - 1):], conv_full[:, ts:])
        for acc, val in zip(outs, per_layer):
            acc.append(val)

    y_prompt = _final_norm(xp, final_norm.reshape(1, d)).reshape(bp, t, d)
    y_sample = _final_norm(xs, final_norm.reshape(1, d)).reshape(bs, ts, d)
    stack_axes = (1, 1, 1, 1, 0, 0, 0, 0, 0, 0, 0, 0)
    return (y_prompt, y_sample) + tuple(jnp.stack(v, axis=ax) for v, ax in zip(outs, stack_axes))
```

```python
import functools

import jax
import jax.numpy as jnp
from jax import lax
from jax.experimental import pallas as pl
from jax.experimental.pallas import tpu as pltpu

F32 = jnp.float32
BF16 = jnp.bfloat16

NORM_EPS = 1e-6
L2_EPS = 1e-6
H_A, D_A = 8, 64
W_A = H_A * D_A
H_B, D_B = 8, 64
W_B = H_B * D_B
LORA_W, LORA_A, LORA_G = 64, 64, 128
R_B = 3 * W_B + LORA_W + LORA_A + LORA_G
LN_X_EPS = D_B * 1e-5
H_C, D_C = 4, 128
W_C = H_C * D_C
CONV_W = 4
LANES = 128
SUBLANES = 8
HEADS_PER_SYSTEM = 4
SYSTEM_ROWS = 256
PROMPT_CHUNK = SYSTEM_ROWS // HEADS_PER_SYSTEM
TM_DENSE = 512
TM_WIDE_PROJ = 256
TQ_ATTN = 256
VMEM_LIMIT_BYTES = 52 * 1024 * 1024


def _cparams(*sem):
    return pltpu.CompilerParams(dimension_semantics=sem, vmem_limit_bytes=VMEM_LIMIT_BYTES)


def _dot(a, b):
    return jnp.dot(a.astype(BF16), b.astype(BF16), preferred_element_type=F32)


def _dot_nt(a, b):
    return lax.dot_general(a.astype(BF16), b.astype(BF16), (((1,), (1,)), ((), ())),
                           preferred_element_type=F32)


def _dot_tn(a, b):
    return lax.dot_general(a.astype(BF16), b.astype(BF16), (((0,), (0,)), ((), ())),
                           preferred_element_type=F32)


def _split(x):
    hi = x.astype(BF16)
    lo = (x - hi.astype(F32)).astype(BF16)
    return hi, lo


def _dot_exact_rhs(a, b_bf16):
    ah, al = _split(a)
    return (jnp.dot(ah, b_bf16, preferred_element_type=F32)
            + jnp.dot(al, b_bf16, preferred_element_type=F32))


def _dot_exact_lhs(a_bf16, b):
    bh, bl = _split(b)
    return (jnp.dot(a_bf16, bh, preferred_element_type=F32)
            + jnp.dot(a_bf16, bl, preferred_element_type=F32))


def _softplus(z):
    return jnp.maximum(z, 0.0) + jnp.log(1.0 + jnp.exp(-jnp.abs(z)))


def _sigmoid(z):
    return 1.0 / (1.0 + jnp.exp(-z))


def _rms(x, w):
    return x * lax.rsqrt(jnp.mean(x * x, axis=-1, keepdims=True) + NORM_EPS) * w


def _iota2(shape, dim):
    return lax.broadcasted_iota(jnp.int32, shape, dim)


def _time_prefix_matrix(m, ch):
    r = _iota2((m, m), 0)
    c = _iota2((m, m), 1)
    return jnp.where(((r // ch) == (c // ch)) & (c <= r), 1.0, 0.0).astype(BF16)


def _ffn_body(x_ref, nw_ref, wg_ref, wu_ref, wd_ref, o_ref, hn_ref, acc_ref):
    j = pl.program_id(1)

    @pl.when(j == 0)
    def _():
        hn_ref[...] = _rms(x_ref[...], nw_ref[...]).astype(BF16)
        acc_ref[...] = jnp.zeros_like(acc_ref)

    h = hn_ref[...]
    g = jnp.dot(h, wg_ref[...], preferred_element_type=F32)
    u = jnp.dot(h, wu_ref[...], preferred_element_type=F32)
    act = (g * _sigmoid(g) * u).astype(BF16)
    acc_ref[...] += jnp.dot(act, wd_ref[...], preferred_element_type=F32)

    @pl.when(j == pl.num_programs(1) - 1)
    def _():
        o_ref[...] = x_ref[...] + 0.5 * acc_ref[...]


def _ffn(x, nw, wg, wu, wd, l):
    m, d = x.shape
    ff = wg.shape[-1]
    tm = min(TM_DENSE, m)
    tf = ff // 2 if (ff // 2) % LANES == 0 else ff
    return pl.pallas_call(
        _ffn_body,
        grid=(m // tm, ff // tf),
        in_specs=[
            pl.BlockSpec((tm, d), lambda i, j: (i, 0)),
            pl.BlockSpec((None, 1, d), lambda i, j: (l, 0, 0)),
            pl.BlockSpec((None, d, tf), lambda i, j: (l, 0, j)),
            pl.BlockSpec((None, d, tf), lambda i, j: (l, 0, j)),
            pl.BlockSpec((None, tf, d), lambda i, j: (l, j, 0)),
        ],
        out_specs=pl.BlockSpec((tm, d), lambda i, j: (i, 0)),
        out_shape=jax.ShapeDtypeStruct((m, d), F32),
        scratch_shapes=[pltpu.VMEM((tm, d), BF16), pltpu.VMEM((tm, d), F32)],
        compiler_params=_cparams("parallel", "arbitrary"),
        name="ffn",
    )(x, nw, wg, wu, wd)


def _proj_body(x_ref, nw_ref, w_ref, *o_refs, widths, sigmoid_last):
    h = _rms(x_ref[...], nw_ref[...]).astype(BF16)
    off = 0
    for idx, (o_ref, wd) in enumerate(zip(o_refs, widths)):
        y = jnp.dot(h, w_ref[:, off:off + wd], preferred_element_type=F32)
        if sigmoid_last and idx == len(widths) - 1:
            y = _sigmoid(y)
        o_ref[...] = y
        off += wd


def _proj(x, nw, w, l, widths, tm, sigmoid_last=False):
    m, d = x.shape
    n = w.shape[-1]
    assert sum(widths) == n
    tm = min(tm, m)
    return pl.pallas_call(
        functools.partial(_proj_body, widths=widths, sigmoid_last=sigmoid_last),
        grid=(m // tm,),
        in_specs=[
            pl.BlockSpec((tm, d), lambda i: (i, 0)),
            pl.BlockSpec((None, 1, d), lambda i: (l, 0, 0)),
            pl.BlockSpec((None, d, n), lambda i: (l, 0, 0)),
        ],
        out_specs=[pl.BlockSpec((tm, wd), lambda i: (i, 0)) for wd in widths],
        out_shape=[jax.ShapeDtypeStruct((m, wd), F32) for wd in widths],
        compiler_params=_cparams("parallel"),
        name="proj",
    )(x, nw, w)


def _merge_body(x_ref, oa_ref, ob_ref, oc_ref, ga_ref, gb_ref, gc_ref,
                wa_ref, wb_ref, wc_ref, wo_ref, o_ref):
    merged = (ga_ref[...] * _dot(oa_ref[...], wa_ref[...])
              + gb_ref[...] * _dot(ob_ref[...], wb_ref[...])
              + gc_ref[...] * _dot(oc_ref[...], wc_ref[...]))
    o_ref[...] = x_ref[...] + _dot(merged, wo_ref[...])


def _merge(x, oa, ob, oc, gates, wa, wb, wc, wo, l):
    m, d = x.shape
    tm = min(TM_DENSE, m)
    row = lambda i: (i, 0)
    wspec = lambda k: pl.BlockSpec((None, k, d), lambda i: (l, 0, 0))
    return pl.pallas_call(
        _merge_body,
        grid=(m // tm,),
        in_specs=[
            pl.BlockSpec((tm, d), row),
            pl.BlockSpec((tm, W_A), row), pl.BlockSpec((tm, W_B), row), pl.BlockSpec((tm, W_C), row),
            pl.BlockSpec((tm, d), lambda i: (i, 0)),
            pl.BlockSpec((tm, d), lambda i: (i, 1)),
            pl.BlockSpec((tm, d), lambda i: (i, 2)),
            wspec(W_A), wspec(W_B), wspec(W_C), wspec(d),
        ],
        out_specs=pl.BlockSpec((tm, d), row),
        out_shape=jax.ShapeDtypeStruct((m, d), F32),
        compiler_params=_cparams("parallel"),
        name="merge",
    )(x, oa, ob, oc, gates, gates, gates, wa, wb, wc, wo)


def _norm_body(x_ref, w_ref, o_ref):
    o_ref[...] = _rms(x_ref[...], w_ref[...])


def _final_norm(x, w):
    m, d = x.shape
    tm = min(TM_DENSE, m)
    return pl.pallas_call(
        _norm_body,
        grid=(m // tm,),
        in_specs=[pl.BlockSpec((tm, d), lambda i: (i, 0)), pl.BlockSpec((1, d), lambda i: (0, 0))],
        out_specs=pl.BlockSpec((tm, d), lambda i: (i, 0)),
        out_shape=jax.ShapeDtypeStruct((m, d), F32),
        compiler_params=_cparams("parallel"),
        name="final_norm",
    )(x, w)


def _sb_sweep(chains, tri):
    sps = []
    for zs, masks, _, _, _ in chains:
        row = []
        for z, mask in zip(zs, masks):
            sp = _softplus(z)
            if mask is not None:
                sp = jnp.where(mask, sp, 0.0)
            row.append(sp.astype(BF16))
        sps.append(row)
    css = [[jnp.dot(sp, tri, preferred_element_type=F32) for sp in row] for row in sps]
    state = [[o, r] for _, _, _, o, r in chains]
    for i in range(max(len(c[0]) for c in chains)):
        for st, (zs, masks, pvs, _, _), row in zip(state, chains, css):
            if i < len(zs):
                a = jnp.exp(zs[i] - (st[1] + row[i]))
                if masks[i] is not None:
                    a = jnp.where(masks[i], a, 0.0)
                st[0] = st[0] + pvs[i](a.astype(BF16))
                st[1] = st[1] + row[i][:, 0:1]
    return tuple((o, r) for o, r in state)


def _suffix_matrix(tk):
    return jnp.where(_iota2((tk, tk), 0) >= _iota2((tk, tk), 1), 1.0, 0.0).astype(BF16)


def _attn_prompt_body(bias_ref, q_ref, k_ref, v_ref, o_ref, kb_ref, vb_ref, *, tq):
    p = pl.program_id(1)
    qi = pl.program_id(2)
    heads = LANES // D_A

    @pl.when(qi == 0)
    def _():
        kb_ref[...] = k_ref[...].astype(BF16)
        vb_ref[...] = v_ref[...].astype(BF16)

    subs = 2
    lane = _iota2((tq, LANES), 1)
    causal = _iota2((tq, tq), 1) < _iota2((tq, tq), 0)
    tri = _suffix_matrix(tq)
    biases = [bias_ref[heads * p + hh] for hh in range(heads)]
    qhs = []
    for s in range(subs):
        q = q_ref[s * tq:(s + 1) * tq, :] * (D_A ** -0.5)
        qhs.append([jnp.where(lane // D_A == hh, q, 0.0).astype(BF16) for hh in range(heads)])

    def sweep(js, plan, carry):
        kv = []
        for j in js:
            start = pl.multiple_of(j * tq, tq)
            kv.append((kb_ref[pl.ds(start, tq), :], vb_ref[pl.ds(start, tq), :]))
        chains = []
        for s in range(subs):
            for hh in range(heads):
                zs = [_dot_nt(qhs[s][hh], kv[i][0]) + biases[hh] for i, _ in plan[s]]
                pvs = [lambda a, vblk=kv[i][1]: jnp.dot(a, vblk, preferred_element_type=F32)
                       for i, _ in plan[s]]
                chains.append((zs, [m for _, m in plan[s]], pvs) + tuple(carry[s * heads + hh]))
        return _sb_sweep(chains, tri)

    init = tuple((jnp.zeros((tq, LANES), F32), jnp.zeros((tq, 1), F32)) for _ in range(subs * heads))
    j_lo = subs * qi
    carry = sweep([j_lo + 1, j_lo], [[(1, causal)], [(0, causal), (1, None)]], init)

    def pair(jj, c):
        j = j_lo - 1 - 2 * jj
        return sweep([j, j - 1], [[(0, None), (1, None)]] * subs, c)

    carry = lax.fori_loop(0, qi, pair, carry)
    for s in range(subs):
        o_ref[s * tq:(s + 1) * tq, :] = jnp.where(lane < D_A, carry[s * heads][0], carry[s * heads + 1][0])


def _attn_prompt(q, k, v, bias, b, t):
    m = b * t
    tq = TQ_ATTN
    rows = 2 * tq
    assert t % rows == 0
    nq = t // rows
    npair = W_A // LANES
    grid_spec = pltpu.PrefetchScalarGridSpec(
        num_scalar_prefetch=1,
        grid=(b, npair, nq),
        in_specs=[
            pl.BlockSpec((rows, LANES), lambda bi, p, qi, bias: (bi * nq + qi, p)),
            pl.BlockSpec((t, LANES), lambda bi, p, qi, bias: (bi, p)),
            pl.BlockSpec((t, LANES), lambda bi, p, qi, bias: (bi, p)),
        ],
        out_specs=pl.BlockSpec((rows, LANES), lambda bi, p, qi, bias: (bi * nq + qi, p)),
        scratch_shapes=[pltpu.VMEM((t, LANES), BF16), pltpu.VMEM((t, LANES), BF16)],
    )
    return pl.pallas_call(
        functools.partial(_attn_prompt_body, tq=tq),
        grid_spec=grid_spec,
        out_shape=jax.ShapeDtypeStruct((m, W_A), F32),
        compiler_params=_cparams("parallel", "parallel", "arbitrary"),
        name="attn_prompt",
    )(bias, q, k, v)


def _attn_sample_body(pt_ref, bias_ref, q_ref, kn_ref, vn_ref, *rest, n_pages, page):
    kp_refs = rest[:n_pages]
    vp_refs = rest[n_pages:2 * n_pages]
    o_ref = rest[2 * n_pages]
    nrow = H_A * SUBLANES
    q8 = q_ref[...] * (D_A ** -0.5)
    head_of_lane = _iota2((SUBLANES, W_A), 1) // D_A
    qm = jnp.concatenate([jnp.where(head_of_lane == h, q8, 0.0) for h in range(H_A)], axis=0).astype(BF16)
    ri = _iota2((nrow, page), 0)
    bias = jnp.zeros((nrow, page), F32)
    for h in range(H_A):
        bias = jnp.where(ri // SUBLANES == h, bias_ref[h], bias)
    mask_new = _iota2((nrow, page), 1) < (ri % SUBLANES)
    tri = _suffix_matrix(page)
    pad = jnp.zeros((page - SUBLANES, W_A), F32)
    k_new = jnp.concatenate([kn_ref[...], pad], axis=0).astype(BF16)
    v_new = jnp.concatenate([vn_ref[...], pad], axis=0).astype(BF16)
    zs = [_dot_nt(qm, k_new) + bias]
    masks = [mask_new]
    pvs = [lambda a: jnp.dot(a, v_new, preferred_element_type=F32)]
    for j in reversed(range(n_pages)):
        zs.append(jnp.dot(qm, kp_refs[j][...].astype(BF16), preferred_element_type=F32) + bias)
        masks.append(None)
        pvs.append(lambda a, j=j: _dot_nt(a, vp_refs[j][...].astype(BF16)))
    chain = (zs, masks, pvs, jnp.zeros((nrow, W_A), F32), jnp.zeros((nrow, 1), F32))
    ((o, _),) = _sb_sweep([chain], tri)
    out = jnp.zeros((SUBLANES, W_A), F32)
    for h in range(H_A):
        out = jnp.where(head_of_lane == h, o[h * SUBLANES:(h + 1) * SUBLANES, :], out)
    o_ref[...] = out


def _attn_sample(q8, k8, v8, cache_kt, cache_vt, page_table, bias, l):
    bs = q8.shape[0]
    n_pages = page_table.shape[1]
    page = cache_kt.shape[3]
    new_spec = pl.BlockSpec((None, SUBLANES, W_A), lambda bi, pt, bias: (bi, 0, 0))

    def page_spec(j):
        return pl.BlockSpec((None, None, W_A, page), lambda bi, pt, bias, j=j: (pt[bi, j], l, 0, 0))

    grid_spec = pltpu.PrefetchScalarGridSpec(
        num_scalar_prefetch=2,
        grid=(bs,),
        in_specs=[new_spec, new_spec, new_spec]
        + [page_spec(j) for j in range(n_pages)] + [page_spec(j) for j in range(n_pages)],
        out_specs=new_spec,
    )
    return pl.pallas_call(
        functools.partial(_attn_sample_body, n_pages=n_pages, page=page),
        grid_spec=grid_spec,
        out_shape=jax.ShapeDtypeStruct((bs, SUBLANES, W_A), F32),
        compiler_params=_cparams("parallel"),
        name="attn_sample",
    )(page_table, bias, q8, k8, v8, *([cache_kt] * n_pages), *([cache_vt] * n_pages))


def _group_sum(x, ones_bd):
    nt = x.shape[1] // LANES
    stacked = jnp.concatenate([x[:, i * LANES:(i + 1) * LANES] for i in range(nt)], axis=0)
    s = _dot_exact_rhs(stacked, ones_bd)
    m = x.shape[0]
    return jnp.concatenate([s[i * m:(i + 1) * m] for i in range(nt)], axis=1)


def _rwkv_body(cols_ref, shift_ref, s0_ref, mu_ref, w0_ref, wwa_ref, a0_ref, g2_ref, kk_ref, ka_ref,
               rk_ref, lnw_ref, lnb_ref, ones_ref, ob_ref, sout_ref, s_ref, prev_ref, *, t_valid, padded):
    c = pl.program_id(1)
    nc = pl.num_programs(1)
    nb, ch, _ = cols_ref.shape
    m = nb * ch
    hps = HEADS_PER_SYSTEM
    grp = hps * D_B
    n_grp = W_B // grp
    sub = hps * ch

    @pl.when(c == 0)
    def _():
        prev_ref[...] = shift_ref[...]
        s_ref[...] = jnp.zeros_like(s_ref)
        for e in range(nb):
            for h in range(H_B):
                g, hh = divmod(h, hps)
                s_ref[e, g, hh * D_B:(hh + 1) * D_B, hh * D_B:(hh + 1) * D_B] = s0_ref[e, h]

    cols3 = cols_ref[...]
    cols = cols3.reshape(m, R_B)
    row = _iota2((m, 1), 0)
    first = jnp.broadcast_to(prev_ref[...], (nb, ch, R_B)).reshape(m, R_B)
    prev = jnp.where(row % ch == 0, first, pltpu.roll(cols, 1, 0))
    prev_ref[...] = cols3[:, ch - 1:ch, :]
    mixed = cols + mu_ref[...] * (prev - cols)
    r = mixed[:, 0:W_B]
    k = mixed[:, W_B:2 * W_B]
    v = mixed[:, 2 * W_B:3 * W_B]
    lora = mixed[:, 3 * W_B:3 * W_B + LORA_W + LORA_A]
    gl = mixed[:, 3 * W_B + LORA_W + LORA_A:]
    lora = jnp.where(_iota2(lora.shape, 1) < LORA_W, jnp.tanh(lora), lora)
    wa = _dot(lora, wwa_ref[...])
    w_log = -_softplus(-(w0_ref[...] + wa[:, :W_B])) - 0.5
    logw = -jnp.exp(w_log)
    a = _sigmoid(a0_ref[...] + wa[:, W_B:])
    g = _dot(_sigmoid(gl), g2_ref[...])
    ones_bd = ones_ref[...]
    kk = k * kk_ref[...]
    k2 = k * (1.0 + (a - 1.0) * ka_ref[...])
    sums = _group_sum(jnp.concatenate([kk * kk, r * k2 * rk_ref[...]], axis=0), ones_bd)
    kk = kk * lax.rsqrt(sums[0:m] + L2_EPS)
    bonus = sums[m:2 * m] * v
    v_s = v
    if padded:
        valid = (row % ch) + c * ch < t_valid
        logw = jnp.where(valid, logw, 0.0)
        kk = jnp.where(valid, kk, 0.0)
        k2 = jnp.where(valid, k2, 0.0)
        v_s = jnp.where(valid, v, 0.0)
    a_s = -kk
    b_s = kk * a

    cum = _dot_exact_lhs(_time_prefix_matrix(m, ch), logw)
    e_in = jnp.exp(cum)
    e_neg = jnp.exp(-cum)
    at = a_s * jnp.exp(cum - logw)
    bt = b_s * e_neg
    kt = k2 * e_neg
    rt = r * e_in

    nbs = SYSTEM_ROWS // sub
    n = nbs * sub
    bm = ((_iota2((n, grp), 0) // ch) % hps) == (_iota2((n, grp), 1) // D_B)
    pr = _iota2((n, n), 0)
    pc = _iota2((n, n), 1)
    same = (pr // ch) == (pc // ch)
    strict = same & (pc < pr)
    incl = same & (pc <= pr)
    n_levels = max(1, (ch - 1).bit_length())

    systems = [(range(e0, e0 + nbs), gi, slice(gi * grp, (gi + 1) * grp))
               for e0 in range(0, nb, nbs) for gi in range(n_grp)]
    per_seq = lambda x: [x[i * sub:(i + 1) * sub] for i in range(nbs)]
    cat = lambda xs: xs[0] if len(xs) == 1 else jnp.concatenate(xs, axis=0)

    def bd(x):
        out = []
        for seqs, _, sl in systems:
            pieces = [x[e * ch:(e + 1) * ch, sl] for e in seqs for _ in range(hps)]
            out.append(jnp.where(bm, jnp.concatenate(pieces, axis=0), 0.0).astype(BF16))
        return out

    ab, bb, kb, rb, vb = (bd(x) for x in (at, bt, kt, rt, v_s))
    ar = [jnp.concatenate([a_, r_], axis=0) for a_, r_ in zip(ab, rb)]
    s_old = [[s_ref[e, gi] for e in seqs] for seqs, gi, _ in systems]
    s16 = [[s.astype(BF16) for s in ss] for ss in s_old]
    ar_b = [_dot_nt(x, y_) for x, y_ in zip(ar, bb)]
    ar_k = [_dot_nt(x, y_) for x, y_ in zip(ar, kb)]
    l_ab = [jnp.where(strict, x[0:n], 0.0) for x in ar_b]
    m_rb = [jnp.where(incl, x[n:2 * n], 0.0).astype(BF16) for x in ar_b]
    l_ak = [jnp.where(strict, x[0:n], 0.0) for x in ar_k]
    m_rk = [jnp.where(incl, x[n:2 * n], 0.0) for x in ar_k]
    a_s0 = [cat([_dot_nt(x, s) for x, s in zip(per_seq(a_), ss)]) for a_, ss in zip(ab, s16)]
    r_s0 = [cat([_dot_nt(x, s) for x, s in zip(per_seq(r_), ss)]) for r_, ss in zip(rb, s16)]
    rhs = [x + _dot(l, v_) for x, l, v_ in zip(a_s0, l_ak, vb)]
    y_v = [x + _dot(mk, v_) for x, mk, v_ in zip(r_s0, m_rk, vb)]
    eye = jnp.where(pr == pc, 1.0, 0.0)
    inv = [eye + l for l in l_ab]
    pw = l_ab
    for _ in range(n_levels - 1):
        pw = [_dot(p_, p_) for p_ in pw]
        inv = [i_ + _dot(i_, p_) for i_, p_ in zip(inv, pw)]
    ub = [_dot(i_, x).astype(BF16) for i_, x in zip(inv, rhs)]
    y_parts = {}
    for si, (seqs, gi, sl) in enumerate(systems):
        y = y_v[si] + jnp.dot(m_rb[si], ub[si], preferred_element_type=F32)
        for e, ye in zip(seqs, per_seq(y)):
            y_parts[e, gi] = sum(ye[h * ch:(h + 1) * ch] for h in range(hps))
        parts = zip(seqs, per_seq(ub[si]), per_seq(bb[si]), per_seq(vb[si]), per_seq(kb[si]), s_old[si])
        for e, ue, be, ve, ke, so in parts:
            wc = e_in[e * ch + ch - 1:e * ch + ch, sl]
            s_ref[e, gi] = (so + _dot_tn(ue, be) + _dot_tn(ve, ke)) * wc
    y = cat([jnp.concatenate([y_parts[e, gi] for gi in range(n_grp)], axis=1) for e in range(nb)])

    inv_d = 1.0 / D_B
    dev = y - _group_sum(y, ones_bd) * inv_d
    var = _group_sum(dev * dev, ones_bd) * inv_d
    yn = dev * lax.rsqrt(var + LN_X_EPS) * lnw_ref[...] + lnb_ref[...]
    ob_ref[...] = ((yn + bonus) * g).reshape(nb, ch, W_B)

    @pl.when(c == nc - 1)
    def _():
        for e in range(nb):
            for h in range(H_B):
                gq, hh = divmod(h, hps)
                sout_ref[e, h] = s_ref[e, gq, hh * D_B:(hh + 1) * D_B, hh * D_B:(hh + 1) * D_B]


def _seqs_per_step(b, ch, heads, wanted_systems):
    nbs = SYSTEM_ROWS // (heads * ch)
    nb = nbs * wanted_systems
    while nb > nbs and b % nb:
        nb -= nbs
    assert b % nb == 0
    return nb


def _rwkv(cols, shift, s0, prm, l, t_valid, ch):
    b, t, _ = cols.shape
    nb = _seqs_per_step(b, ch, HEADS_PER_SYSTEM, 2 if t > ch else 1)
    nc = t // ch
    assert t % ch == 0 and (ch == PROMPT_CHUNK or nc == 1)
    vec = lambda n: pl.BlockSpec((None, 1, n), lambda bi, c: (l, 0, 0))
    mat = lambda k, n: pl.BlockSpec((None, k, n), lambda bi, c: (l, 0, 0))
    return pl.pallas_call(
        functools.partial(_rwkv_body, t_valid=t_valid, padded=t_valid < t),
        grid=(b // nb, nc),
        in_specs=[
            pl.BlockSpec((nb, ch, R_B), lambda bi, c: (bi, c, 0)),
            pl.BlockSpec((nb, 1, R_B), lambda bi, c: (bi, 0, 0)),
            pl.BlockSpec((nb, H_B, D_B, D_B), lambda bi, c: (bi, 0, 0, 0)),
            vec(R_B), vec(W_B), mat(LORA_W + LORA_A, 2 * W_B), vec(W_B), mat(LORA_G, W_B),
            vec(W_B), vec(W_B), vec(W_B), vec(W_B), vec(W_B),
            pl.BlockSpec((LANES, LANES), lambda bi, c: (0, 0)),
        ],
        out_specs=[
            pl.BlockSpec((nb, ch, W_B), lambda bi, c: (bi, c, 0)),
            pl.BlockSpec((nb, H_B, D_B, D_B), lambda bi, c: (bi, 0, 0, 0)),
        ],
        out_shape=[jax.ShapeDtypeStruct((b, t, W_B), F32),
                   jax.ShapeDtypeStruct((b, H_B, D_B, D_B), F32)],
        scratch_shapes=[pltpu.VMEM((nb, W_B // (HEADS_PER_SYSTEM * D_B), HEADS_PER_SYSTEM * D_B,
                                    HEADS_PER_SYSTEM * D_B), F32),
                        pltpu.VMEM((nb, 1, R_B), F32)],
        compiler_params=_cparams("parallel", "arbitrary"),
        name="rwkv7",
    )(cols, shift, s0, prm["mu"], prm["w0"], prm["wwa"], prm["a0"], prm["g2"], prm["k_k"], prm["k_a"],
      prm["r_k"], prm["ln_w"], prm["ln_b"], prm["ones_bd"])


def _gdn_body(x_ref, ab_ref, z_ref, cprev_ref, s0_ref, cw_ref, alog_ref, dtb_ref, nw_ref,
              oc_ref, sout_ref, s_ref, xbuf_ref, *, t_valid, padded):
    c = pl.program_id(1)
    nc = pl.num_programs(1)
    nb, ch, _ = x_ref.shape
    m = nb * ch
    hist = SUBLANES

    @pl.when(c == 0)
    def _():
        xbuf_ref[:, 0:hist, :] = cprev_ref[...]
        s_ref[...] = s0_ref[...]

    x3 = x_ref[...]
    xbuf_ref[:, hist:hist + ch, :] = x3
    cw = cw_ref[...]
    conv = x3.reshape(m, 3 * W_C) * cw[CONV_W - 1:CONV_W, :]
    for i in range(1, CONV_W):
        conv = conv + xbuf_ref[:, hist - i:hist - i + ch, :].reshape(m, 3 * W_C) * cw[CONV_W - 1 - i:CONV_W - i, :]
    conv = conv * _sigmoid(conv)
    xbuf_ref[:, 0:hist, :] = x3[:, ch - hist:ch, :]

    def stack(y):
        return jnp.concatenate([y[e * ch:(e + 1) * ch, h * D_C:(h + 1) * D_C]
                                for e in range(nb) for h in range(H_C)], axis=0)

    q = stack(conv[:, 0:W_C])
    k = stack(conv[:, W_C:2 * W_C])
    v = stack(conv[:, 2 * W_C:3 * W_C])
    q = q * lax.rsqrt(jnp.sum(q * q, axis=-1, keepdims=True) + L2_EPS) * (D_C ** -0.5)
    k = k * lax.rsqrt(jnp.sum(k * k, axis=-1, keepdims=True) + L2_EPS)

    ab = ab_ref[...].reshape(m, LANES)
    g_all = -jnp.exp(alog_ref[...]) * _softplus(ab + dtb_ref[...])
    beta_all = _sigmoid(ab)
    if padded:
        valid = (_iota2((m, 1), 0) % ch) + c * ch < t_valid
        g_all = jnp.where(valid, g_all, 0.0)
        beta_all = jnp.where(valid, beta_all, 0.0)
    cum_all = _dot_exact_lhs(_time_prefix_matrix(m, ch), g_all)

    def rows(src, lane0, last_only=False):
        out = []
        for e in range(nb):
            for h in range(H_C):
                lo = e * ch + (ch - 1 if last_only else 0)
                piece = src[lo:(e + 1) * ch, lane0 + h:lane0 + h + 1]
                out.append(jnp.broadcast_to(piece, (ch, D_C)))
        return jnp.concatenate(out, axis=0)

    cum = rows(cum_all, 0)
    beta = rows(beta_all, H_C)
    g_last = rows(cum_all, 0, last_only=True)

    n = SYSTEM_ROWS
    n_sys = nb * H_C * ch // n
    per_sys = lambda y: [y[i * n:(i + 1) * n] for i in range(n_sys)]
    pr = _iota2((n, n), 0)
    pc = _iota2((n, n), 1)
    same = (pr // ch) == (pc // ch)
    strict = same & (pc < pr)
    incl = same & (pc <= pr)
    decay = []
    for cum_s in per_sys(cum):
        cum_col = jnp.concatenate([cum_s] * (n // D_C), axis=1)
        cum_row = jnp.broadcast_to(jnp.transpose(cum_s)[0:1, :], (n, n))
        decay.append(jnp.where(incl, jnp.exp(jnp.where(incl, cum_col - cum_row, 0.0)), 0.0))

    kbeta = k * beta
    kb16 = k.astype(BF16)
    e_cum = jnp.exp(cum)
    kk_t = [_dot_nt(x, y) for x, y in zip(per_sys(kbeta), per_sys(kb16))]
    qk_t = [_dot_nt(x, y) for x, y in zip(per_sys(q), per_sys(kb16))]
    neg_a = [jnp.where(strict, -(x * d), 0.0) for x, d in zip(kk_t, decay)]
    eye = jnp.where(pr == pc, 1.0, 0.0)
    inv = [eye + l_ for l_ in neg_a]
    pw = neg_a
    for _ in range(max(1, (ch - 1).bit_length()) - 1):
        pw = [_dot(p_, p_) for p_ in pw]
        inv = [i_ + _dot(i_, p_) for i_, p_ in zip(inv, pw)]
    vk = jnp.concatenate([v * beta, kbeta * e_cum], axis=1)
    uw = jnp.concatenate([_dot(i_, x) for i_, x in zip(inv, per_sys(vk))], axis=0)
    u_cap = uw[:, :D_C]
    w_cap = uw[:, D_C:]
    qk = [x * d for x, d in zip(qk_t, decay)]
    q_dec = q * e_cum
    k_dec = k * jnp.exp(g_last - cum)

    heads = [(e, h, slice((e * H_C + h) * ch, (e * H_C + h + 1) * ch)) for e in range(nb) for h in range(H_C)]
    s_old = [s_ref[e, h] for e, h, _ in heads]
    ws = [_dot(jnp.concatenate([w_cap[sl], q_dec[sl]], axis=0), s) for (_, _, sl), s in zip(heads, s_old)]
    v_new = [u_cap[sl] - w[0:ch] for (_, _, sl), w in zip(heads, ws)]
    o_state = [w[ch:2 * ch] for w in ws]
    for (e, h, sl), s, vn in zip(heads, s_old, v_new):
        s_ref[e, h] = s * jnp.exp(g_last[sl.start:sl.start + 1, :]) + _dot_tn(k_dec[sl], vn)
    vn_all = jnp.concatenate(v_new, axis=0)
    o = jnp.concatenate(o_state, axis=0) + jnp.concatenate(
        [_dot(x, y) for x, y in zip(qk, per_sys(vn_all))], axis=0)

    z3 = z_ref[...]
    for e in range(nb):
        for h in range(H_C):
            i = e * H_C + h
            oh = o[i * ch:(i + 1) * ch]
            oh = oh * lax.rsqrt(jnp.mean(oh * oh, axis=-1, keepdims=True) + NORM_EPS) * nw_ref[...]
            zh = z3[e, :, h * D_C:(h + 1) * D_C]
            oc_ref[e, :, h * D_C:(h + 1) * D_C] = oh * (zh * _sigmoid(zh))

    @pl.when(c == nc - 1)
    def _():
        sout_ref[...] = s_ref[...]


def _gdn(x, ab, z, cprev, s0, prm, l, t_valid, ch):
    b, t, _ = x.shape
    nb = _seqs_per_step(b, ch, H_C, 4 if t > ch else 1)
    nc = t // ch
    assert t % ch == 0 and ch >= SUBLANES
    vec = lambda n: pl.BlockSpec((None, 1, n), lambda bi, c: (l, 0, 0))
    return pl.pallas_call(
        functools.partial(_gdn_body, t_valid=t_valid, padded=t_valid < t),
        grid=(b // nb, nc),
        in_specs=[
            pl.BlockSpec((nb, ch, 3 * W_C), lambda bi, c: (bi, c, 0)),
            pl.BlockSpec((nb, ch, LANES), lambda bi, c: (bi, c, 0)),
            pl.BlockSpec((nb, ch, W_C), lambda bi, c: (bi, c, 0)),
            pl.BlockSpec((nb, SUBLANES, 3 * W_C), lambda bi, c: (bi, 0, 0)),
            pl.BlockSpec((nb, H_C, D_C, D_C), lambda bi, c: (bi, 0, 0, 0)),
            pl.BlockSpec((None, CONV_W, 3 * W_C), lambda bi, c: (l, 0, 0)),
            vec(LANES), vec(LANES), vec(D_C),
        ],
        out_specs=[
            pl.BlockSpec((nb, ch, W_C), lambda bi, c: (bi, c, 0)),
            pl.BlockSpec((nb, H_C, D_C, D_C), lambda bi, c: (bi, 0, 0, 0)),
        ],
        out_shape=[jax.ShapeDtypeStruct((b, t, W_C), F32),
                   jax.ShapeDtypeStruct((b, H_C, D_C, D_C), F32)],
        scratch_shapes=[pltpu.VMEM((nb, H_C, D_C, D_C), F32),
                        pltpu.VMEM((nb, SUBLANES + ch, 3 * W_C), F32)],
        compiler_params=_cparams("parallel", "arbitrary"),
        name="gdn",
    )(x, ab, z, cprev, s0, prm["conv_w"], prm["a_log"], prm["dt_bias"], prm["norm_w"])


def _pad_time(x, t_to):
    return jnp.pad(x, ((0, 0), (0, t_to - x.shape[1]), (0, 0)))


def kernel(x_prompt, x_sample, cache_k, cache_v, page_table, state_rwkv, state_rwkv_shift, state_gdn,
           state_gdn_conv, ffn1_norm, ffn1_w_gate, ffn1_w_up, ffn1_w_down, mix_norm, w_in, sb_bias,
           rwkv_mu, rwkv_w0, rwkv_w2, rwkv_a0, rwkv_a2, rwkv_g2, rwkv_k_k, rwkv_k_a, rwkv_r_k,
           rwkv_ln_w, rwkv_ln_b, gdn_conv_w, gdn_a_log, gdn_dt_bias, gdn_norm_w,
           w_br_a, w_br_b, w_br_c, w_out, ffn2_norm, ffn2_w_gate, ffn2_w_up, ffn2_w_down, final_norm):
    bp, t, d = x_prompt.shape
    bs, ts, _ = x_sample.shape
    n_layers = ffn1_norm.shape[0]
    n_pool, _, page, _, _ = cache_k.shape
    assert t % PROMPT_CHUNK == 0 and ts <= SUBLANES

    bf = lambda w: w.astype(BF16)
    row3 = lambda w: w.reshape(n_layers, 1, -1)
    f1g, f1u, f1d = bf(ffn1_w_gate), bf(ffn1_w_up), bf(ffn1_w_down)
    f2g, f2u, f2d = bf(ffn2_w_gate), bf(ffn2_w_up), bf(ffn2_w_down)
    o_b = 3 * W_A
    o_c = o_b + R_B
    o_ab = o_c + 3 * W_C
    o_z = o_ab + 2 * H_C
    w1 = bf(w_in[:, :, :o_c])
    w2 = bf(jnp.concatenate([
        w_in[:, :, o_c:o_ab],
        jnp.pad(w_in[:, :, o_ab:o_z], ((0, 0), (0, 0), (0, LANES - 2 * H_C))),
        w_in[:, :, o_z:], ], axis=-1))
    widths1 = (W_A, W_A, W_A, R_B)
    widths2 = (3 * W_C, LANES, W_C, 3 * d)
    wba, wbb, wbc, wo = bf(w_br_a), bf(w_br_b), bf(w_br_c), bf(w_out)
    zeros_wa = jnp.zeros((n_layers, LORA_W, W_B), F32)
    wwa = bf(jnp.concatenate([
        jnp.concatenate([rwkv_w2, zeros_wa], axis=-1),
        jnp.concatenate([zeros_wa, rwkv_a2], axis=-1)], axis=1))
    gi = jnp.arange(LANES) // D_B
    rwkv_prm = dict(
        mu=row3(rwkv_mu), w0=row3(rwkv_w0), wwa=wwa, a0=row3(rwkv_a0), g2=bf(rwkv_g2),
        k_k=row3(rwkv_k_k), k_a=row3(rwkv_k_a), r_k=row3(rwkv_r_k), ln_w=row3(rwkv_ln_w),
        ln_b=row3(rwkv_ln_b), ones_bd=(gi[:, None] == gi[None, :]).astype(BF16))
    lane_pad = lambda w: jnp.pad(w, ((0, 0), (0, LANES - w.shape[1]))).reshape(n_layers, 1, LANES)
    gdn_prm = dict(conv_w=gdn_conv_w, a_log=lane_pad(gdn_a_log), dt_bias=lane_pad(gdn_dt_bias),
                   norm_w=row3(gdn_norm_w))
    n1a, n1b, n2 = row3(ffn1_norm), row3(mix_norm), row3(ffn2_norm)
    cache_kt = jnp.transpose(cache_k, (0, 1, 3, 4, 2)).reshape(n_pool, n_layers, W_A, page)
    cache_vt = jnp.transpose(cache_v, (0, 1, 3, 4, 2)).reshape(n_pool, n_layers, W_A, page)

    xp = x_prompt.reshape(bp * t, d)
    xs = x_sample.reshape(bs * ts, d)
    zero_shift = jnp.zeros((bp, 1, R_B), F32)
    zero_rwkv = jnp.zeros((bp, H_B, D_B, D_B), F32)
    zero_conv = jnp.zeros((bp, SUBLANES, 3 * W_C), F32)
    zero_gdn = jnp.zeros((bp, H_C, D_C, D_C), F32)
    pad8 = lambda y: _pad_time(y.reshape(bs, ts, -1), SUBLANES)

    outs = [[] for _ in range(12)]
    for l in range(n_layers):
        xp = _ffn(xp, n1a, f1g, f1u, f1d, l)
        xs = _ffn(xs, n1a, f1g, f1u, f1d, l)
        qp, kp, vp, bcp = _proj(xp, n1b, w1, l, widths1, TM_DENSE)
        cxp, cabp, czp, gtp = _proj(xp, n1b, w2, l, widths2, TM_WIDE_PROJ, sigmoid_last=True)
        qs, ks, vs, bcs = _proj(xs, n1b, w1, l, widths1, TM_DENSE)
        cxs, cabs, czs, gts = _proj(xs, n1b, w2, l, widths2, TM_WIDE_PROJ, sigmoid_last=True)

        oap = _attn_prompt(qp, kp, vp, sb_bias[l], bp, t)
        oas = _attn_sample(pad8(qs), pad8(ks), pad8(vs), cache_kt, cache_vt, page_table, sb_bias[l], l)
        oas = oas[:, :ts].reshape(bs * ts, W_A)

        bcp3 = bcp.reshape(bp, t, R_B)
        obp, rwp = _rwkv(bcp3, zero_shift, zero_rwkv, rwkv_prm, l, t, PROMPT_CHUNK)
        bcs3 = bcs.reshape(bs, ts, R_B)
        obs, rws = _rwkv(pad8(bcs), state_rwkv_shift[l][:, None, :], state_rwkv[l], rwkv_prm, l, ts, SUBLANES)
        obs = obs[:, :ts].reshape(bs * ts, W_B)

        cxp3 = cxp.reshape(bp, t, 3 * W_C)
        ocp, gdp = _gdn(cxp3, cabp.reshape(bp, t, LANES), czp.reshape(bp, t, W_C), zero_conv, zero_gdn,
                        gdn_prm, l, t, PROMPT_CHUNK)
        cxs3 = cxs.reshape(bs, ts, 3 * W_C)
        conv_full = jnp.concatenate([state_gdn_conv[l], cxs3], axis=1)
        cprev_s = jnp.pad(state_gdn_conv[l], ((0, 0), (SUBLANES - (CONV_W - 1), 0), (0, 0)))
        ocs, gds = _gdn(pad8(cxs), pad8(cabs), pad8(czs), cprev_s, state_gdn[l], gdn_prm, l, ts, SUBLANES)
        ocs = ocs[:, :ts].reshape(bs * ts, W_C)

        xp = _merge(xp, oap, obp.reshape(bp * t, W_B), ocp.reshape(bp * t, W_C), gtp, wba, wbb, wbc, wo, l)
        xs = _merge(xs, oas, obs, ocs, gts, wba, wbb, wbc, wo, l)
        xp = _ffn(xp, n2, f2g, f2u, f2d, l)
        xs = _ffn(xs, n2, f2g, f2u, f2d, l)

        per_layer = (kp.reshape(bp, t, H_A, D_A), vp.reshape(bp, t, H_A, D_A),
                     ks.reshape(bs, ts, H_A, D_A), vs.reshape(bs, ts, H_A, D_A),
                     rwp, rws, bcp3[:, t - 1], bcs3[:, ts - 1], gdp, gds,
                     cxp3[:, t - (CONV_W - 1):], conv_full[:, ts:])
        for acc, val in zip(outs, per_layer):
            acc.append(val)

    y_prompt = _final_norm(xp, final_norm.reshape(1, d)).reshape(bp, t, d)
    y_sample = _final_norm(xs, final_norm.reshape(1, d)).reshape(bs, ts, d)
    stack_axes = (1, 1, 1, 1, 0, 0, 0, 0, 0, 0, 0, 0)
    return (y_prompt, y_sample) + tuple(jnp.stack(v, axis=ax) for v, ax in zip(outs, stack_axes))
```

```python
import functools

import jax
import jax.numpy as jnp
from jax import lax
from jax.experimental import pallas as pl
from jax.experimental.pallas import tpu as pltpu

F32 = jnp.float32
BF16 = jnp.bfloat16

NORM_EPS = 1e-6
L2_EPS = 1e-6
LOG2_E = 1.4426950408889634
H_A, D_A = 8, 64
W_A = H_A * D_A
H_B, D_B = 8, 64
W_B = H_B * D_B
LORA_W, LORA_A, LORA_G = 64, 64, 128
R_B = 3 * W_B + LORA_W + LORA_A + LORA_G
LN_X_EPS = D_B * 1e-5
H_C, D_C = 4, 128
W_C = H_C * D_C
CONV_W = 4
LANES = 128
SUBLANES = 8
HEADS_PER_SYSTEM = 4
SYSTEM_ROWS = 256
PROMPT_CHUNK = SYSTEM_ROWS // HEADS_PER_SYSTEM
TM_DENSE = 512
TM_WIDE_PROJ = 512
TQ_ATTN = 256
ATTN_PIPELINE_LAG = 2
SAMPLE_PIPELINE_LAG = 4
VMEM_LIMIT_BYTES = 52 * 1024 * 1024


def _cparams(*sem):
    return pltpu.CompilerParams(dimension_semantics=sem, vmem_limit_bytes=VMEM_LIMIT_BYTES)


def _dot(a, b):
    return jnp.dot(a.astype(BF16), b.astype(BF16), preferred_element_type=F32)


def _dot_nt(a, b):
    return lax.dot_general(a.astype(BF16), b.astype(BF16), (((1,), (1,)), ((), ())),
                           preferred_element_type=F32)


def _dot_tn(a, b):
    return lax.dot_general(a.astype(BF16), b.astype(BF16), (((0,), (0,)), ((), ())),
                           preferred_element_type=F32)


def _split(x):
    hi = x.astype(BF16)
    lo = (x - hi.astype(F32)).astype(BF16)
    return hi, lo


def _dot_exact_rhs(a, b_bf16):
    ah, al = _split(a)
    return (jnp.dot(ah, b_bf16, preferred_element_type=F32)
            + jnp.dot(al, b_bf16, preferred_element_type=F32))


def _dot_exact_lhs(a_bf16, b):
    bh, bl = _split(b)
    return (jnp.dot(a_bf16, bh, preferred_element_type=F32)
            + jnp.dot(a_bf16, bl, preferred_element_type=F32))


def _softplus(z):
    return jnp.maximum(z, 0.0) + jnp.log(1.0 + jnp.exp(-jnp.abs(z)))


def _softplus_log2(zl):
    sign = jnp.uint32(0x80000000)
    neg_abs = lax.bitcast_convert_type(lax.bitcast_convert_type(zl, jnp.uint32) | sign, F32)
    return jnp.maximum(zl, 0.0) + jnp.log2(1.0 + jnp.exp2(neg_abs))


def _sigmoid(z):
    return 1.0 / (1.0 + jnp.exp(-z))


def _rms(x, w):
    return x * lax.rsqrt(jnp.mean(x * x, axis=-1, keepdims=True) + NORM_EPS) * w


def _iota2(shape, dim):
    return lax.broadcasted_iota(jnp.int32, shape, dim)


def _time_prefix_matrix(m, ch):
    r = _iota2((m, m), 0)
    c = _iota2((m, m), 1)
    return jnp.where(((r // ch) == (c // ch)) & (c <= r), 1.0, 0.0).astype(BF16)


def _ffn_body(x_ref, nw_ref, wg_ref, wu_ref, wd_ref, o_ref):
    x = x_ref[...]
    h = _rms(x, nw_ref[...]).astype(BF16)
    g = jnp.dot(h, wg_ref[...], preferred_element_type=F32)
    u = jnp.dot(h, wu_ref[...], preferred_element_type=F32)
    act = (g * _sigmoid(g) * u).astype(BF16)
    o_ref[...] = x + 0.5 * jnp.dot(act, wd_ref[...], preferred_element_type=F32)


def _ffn(x, nw, wg, wu, wd, l):
    m, d = x.shape
    ff = wg.shape[-1]
    tm = min(TM_DENSE, m)
    once = pl.Buffered(1)
    return pl.pallas_call(
        _ffn_body,
        grid=(m // tm,),
        in_specs=[
            pl.BlockSpec((tm, d), lambda i: (i, 0)),
            pl.BlockSpec((None, 1, d), lambda i: (l, 0, 0)),
            pl.BlockSpec((None, d, ff), lambda i: (l, 0, 0), pipeline_mode=once),
            pl.BlockSpec((None, d, ff), lambda i: (l, 0, 0), pipeline_mode=once),
            pl.BlockSpec((None, ff, d), lambda i: (l, 0, 0), pipeline_mode=once),
        ],
        out_specs=pl.BlockSpec((tm, d), lambda i: (i, 0)),
        out_shape=jax.ShapeDtypeStruct((m, d), F32),
        compiler_params=_cparams("parallel"),
        name="ffn",
    )(x, nw, wg, wu, wd)


def _proj_body(x_ref, nw_ref, w_ref, *o_refs, widths, sigmoid_last):
    h = _rms(x_ref[...], nw_ref[...]).astype(BF16)
    off = 0
    for idx, (o_ref, wd) in enumerate(zip(o_refs, widths)):
        y = jnp.dot(h, w_ref[:, off:off + wd], preferred_element_type=F32)
        if sigmoid_last and idx == len(widths) - 1:
            y = _sigmoid(y)
        o_ref[...] = y
        off += wd


def _proj(x, nw, w, l, widths, tm, sigmoid_last=False):
    m, d = x.shape
    n = w.shape[-1]
    assert sum(widths) == n
    tm = min(tm, m)
    return pl.pallas_call(
        functools.partial(_proj_body, widths=widths, sigmoid_last=sigmoid_last),
        grid=(m // tm,),
        in_specs=[
            pl.BlockSpec((tm, d), lambda i: (i, 0)),
            pl.BlockSpec((None, 1, d), lambda i: (l, 0, 0)),
            pl.BlockSpec((None, d, n), lambda i: (l, 0, 0), pipeline_mode=pl.Buffered(1)),
        ],
        out_specs=[pl.BlockSpec((tm, wd), lambda i: (i, 0)) for wd in widths],
        out_shape=[jax.ShapeDtypeStruct((m, wd), F32) for wd in widths],
        compiler_params=_cparams("parallel"),
        name="proj",
    )(x, nw, w)


def _merge_body(x_ref, oa_ref, ob_ref, oc_ref, ga_ref, gb_ref, gc_ref,
                wa_ref, wb_ref, wc_ref, wo_ref, o_ref):
    merged = (ga_ref[...] * _dot(oa_ref[...], wa_ref[...])
              + gb_ref[...] * _dot(ob_ref[...], wb_ref[...])
              + gc_ref[...] * _dot(oc_ref[...], wc_ref[...]))
    o_ref[...] = x_ref[...] + _dot(merged, wo_ref[...])


def _merge(x, oa, ob, oc, gates, wa, wb, wc, wo, l):
    m, d = x.shape
    tm = min(TM_DENSE, m)
    row = lambda i: (i, 0)
    wspec = lambda k: pl.BlockSpec((None, k, d), lambda i: (l, 0, 0))
    return pl.pallas_call(
        _merge_body,
        grid=(m // tm,),
        in_specs=[
            pl.BlockSpec((tm, d), row),
            pl.BlockSpec((tm, W_A), row), pl.BlockSpec((tm, W_B), row), pl.BlockSpec((tm, W_C), row),
            pl.BlockSpec((tm, d), lambda i: (i, 0)),
            pl.BlockSpec((tm, d), lambda i: (i, 1)),
            pl.BlockSpec((tm, d), lambda i: (i, 2)),
            wspec(W_A), wspec(W_B), wspec(W_C), wspec(d),
        ],
        out_specs=pl.BlockSpec((tm, d), row),
        out_shape=jax.ShapeDtypeStruct((m, d), F32),
        compiler_params=_cparams("parallel"),
        name="merge",
    )(x, oa, ob, oc, gates, gates, gates, wa, wb, wc, wo)


def _norm_body(x_ref, w_ref, o_ref):
    o_ref[...] = _rms(x_ref[...], w_ref[...])


def _final_norm(x, w):
    m, d = x.shape
    tm = min(TM_DENSE, m)
    return pl.pallas_call(
        _norm_body,
        grid=(m // tm,),
        in_specs=[pl.BlockSpec((tm, d), lambda i: (i, 0)), pl.BlockSpec((1, d), lambda i: (0, 0))],
        out_specs=pl.BlockSpec((tm, d), lambda i: (i, 0)),
        out_shape=jax.ShapeDtypeStruct((m, d), F32),
        compiler_params=_cparams("parallel"),
        name="final_norm",
    )(x, w)


def _sb_sweep(chains, tri, lag):
    order = [(c, i) for i in range(max(len(ch[0]) for ch in chains))
             for c, ch in enumerate(chains) if i < len(ch[0])]
    state = [[o, r] for _, _, _, o, r in chains]
    z, cs = {}, {}
    for step in range(len(order) + 2 * lag):
        if step < len(order):
            c, i = order[step]
            z[c, i] = chains[c][0][i]()
        if 0 <= step - lag < len(order):
            c, i = order[step - lag]
            sp = _softplus_log2(z[c, i])
            if chains[c][1][i] is not None:
                sp = jnp.where(chains[c][1][i], sp, 0.0)
            cs[c, i] = jnp.dot(sp.astype(BF16), tri, preferred_element_type=F32)
        if 0 <= step - 2 * lag < len(order):
            c, i = order[step - 2 * lag]
            st = state[c]
            a = jnp.exp2(z.pop((c, i)) - (st[1] + cs[c, i]))
            if chains[c][1][i] is not None:
                a = jnp.where(chains[c][1][i], a, 0.0)
            st[0] = st[0] + chains[c][2][i](a.astype(BF16))
            st[1] = st[1] + cs.pop((c, i))[:, 0:1]
    return tuple((o, r) for o, r in state)


def _suffix_matrix(tk):
    return jnp.where(_iota2((tk, tk), 0) >= _iota2((tk, tk), 1), 1.0, 0.0).astype(BF16)


def _attn_prompt_body(bias_ref, q_ref, k_ref, v_ref, o_ref, kb_ref, vb_ref, *, tq):
    p = pl.program_id(1)
    qi = pl.program_id(2)
    heads = LANES // D_A

    @pl.when(qi == 0)
    def _():
        kb_ref[...] = k_ref[...].astype(BF16)
        vb_ref[...] = v_ref[...].astype(BF16)

    subs = 2
    lane = _iota2((tq, LANES), 1)
    causal = _iota2((tq, tq), 1) < _iota2((tq, tq), 0)
    tri = _suffix_matrix(tq)
    biases = [bias_ref[heads * p + hh] * LOG2_E for hh in range(heads)]
    qhs = []
    for s in range(subs):
        q = q_ref[s * tq:(s + 1) * tq, :] * (D_A ** -0.5 * LOG2_E)
        qhs.append([jnp.where(lane // D_A == hh, q, 0.0).astype(BF16) for hh in range(heads)])

    def sweep(js, plan, carry):
        kv = []
        for j in js:
            start = pl.multiple_of(j * tq, tq)
            kv.append((kb_ref[pl.ds(start, tq), :], vb_ref[pl.ds(start, tq), :]))
        chains = []
        for s in range(subs):
            for hh in range(heads):
                zs = [lambda qh=qhs[s][hh], kblk=kv[i][0], b=biases[hh]: _dot_nt(qh, kblk) + b
                      for i, _ in plan[s]]
                pvs = [lambda a, vblk=kv[i][1]: jnp.dot(a, vblk, preferred_element_type=F32)
                       for i, _ in plan[s]]
                chains.append((zs, [m for _, m in plan[s]], pvs) + tuple(carry[s * heads + hh]))
        return _sb_sweep(chains, tri, ATTN_PIPELINE_LAG)

    init = tuple((jnp.zeros((tq, LANES), F32), jnp.zeros((tq, 1), F32)) for _ in range(subs * heads))
    j_lo = subs * qi
    carry = sweep([j_lo + 1, j_lo], [[(1, causal)], [(0, causal), (1, None)]], init)

    def pair(jj, c):
        j = j_lo - 1 - 2 * jj
        return sweep([j, j - 1], [[(0, None), (1, None)]] * subs, c)

    carry = lax.fori_loop(0, qi, pair, carry)
    for s in range(subs):
        o_ref[s * tq:(s + 1) * tq, :] = jnp.where(lane < D_A, carry[s * heads][0], carry[s * heads + 1][0])


def _attn_prompt(q, k, v, bias, b, t):
    m = b * t
    tq = TQ_ATTN
    rows = 2 * tq
    assert t % rows == 0
    nq = t // rows
    npair = W_A // LANES
    grid_spec = pltpu.PrefetchScalarGridSpec(
        num_scalar_prefetch=1,
        grid=(b, npair, nq),
        in_specs=[
            pl.BlockSpec((rows, LANES), lambda bi, p, qi, bias: (bi * nq + qi, p)),
            pl.BlockSpec((t, LANES), lambda bi, p, qi, bias: (bi, p)),
            pl.BlockSpec((t, LANES), lambda bi, p, qi, bias: (bi, p)),
        ],
        out_specs=pl.BlockSpec((rows, LANES), lambda bi, p, qi, bias: (bi * nq + qi, p)),
        scratch_shapes=[pltpu.VMEM((t, LANES), BF16), pltpu.VMEM((t, LANES), BF16)],
    )
    return pl.pallas_call(
        functools.partial(_attn_prompt_body, tq=tq),
        grid_spec=grid_spec,
        out_shape=jax.ShapeDtypeStruct((m, W_A), F32),
        compiler_params=_cparams("parallel", "parallel", "arbitrary"),
        name="attn_prompt",
    )(bias, q, k, v)


def _attn_sample_body(pt_ref, bias_ref, q_ref, kn_ref, vn_ref, *rest, n_pages, page):
    kp_refs = rest[:n_pages]
    vp_refs = rest[n_pages:2 * n_pages]
    o_ref = rest[2 * n_pages]
    nrow = H_A * SUBLANES
    q8 = q_ref[...] * (D_A ** -0.5 * LOG2_E)
    head_of_lane = _iota2((SUBLANES, W_A), 1) // D_A
    qm = jnp.concatenate([jnp.where(head_of_lane == h, q8, 0.0) for h in range(H_A)], axis=0).astype(BF16)
    ri = _iota2((nrow, page), 0)
    bias = jnp.zeros((nrow, page), F32)
    for h in range(H_A):
        bias = jnp.where(ri // SUBLANES == h, bias_ref[h] * LOG2_E, bias)
    mask_new = _iota2((nrow, page), 1) < (ri % SUBLANES)
    tri = _suffix_matrix(page)
    pad = jnp.zeros((page - SUBLANES, W_A), F32)
    k_new = jnp.concatenate([kn_ref[...], pad], axis=0).astype(BF16)
    v_new = jnp.concatenate([vn_ref[...], pad], axis=0).astype(BF16)
    zs = [lambda: _dot_nt(qm, k_new) + bias]
    masks = [mask_new]
    pvs = [lambda a: jnp.dot(a, v_new, preferred_element_type=F32)]
    for j in reversed(range(n_pages)):
        zs.append(lambda j=j: jnp.dot(qm, kp_refs[j][...].astype(BF16), preferred_element_type=F32) + bias)
        masks.append(None)
        pvs.append(lambda a, j=j: _dot_nt(a, vp_refs[j][...].astype(BF16)))
    chain = (zs, masks, pvs, jnp.zeros((nrow, W_A), F32), jnp.zeros((nrow, 1), F32))
    ((o, _),) = _sb_sweep([chain], tri, SAMPLE_PIPELINE_LAG)
    out = jnp.zeros((SUBLANES, W_A), F32)
    for h in range(H_A):
        out = jnp.where(head_of_lane == h, o[h * SUBLANES:(h + 1) * SUBLANES, :], out)
    o_ref[...] = out


def _attn_sample(q8, k8, v8, cache_kt, cache_vt, page_table, bias, l):
    bs = q8.shape[0]
    n_pages = page_table.shape[1]
    page = cache_kt.shape[3]
    new_spec = pl.BlockSpec((None, SUBLANES, W_A), lambda bi, pt, bias: (bi, 0, 0))

    def page_spec(j):
        return pl.BlockSpec((None, None, W_A, page), lambda bi, pt, bias, j=j: (pt[bi, j], l, 0, 0))

    grid_spec = pltpu.PrefetchScalarGridSpec(
        num_scalar_prefetch=2,
        grid=(bs,),
        in_specs=[new_spec, new_spec, new_spec]
        + [page_spec(j) for j in range(n_pages)] + [page_spec(j) for j in range(n_pages)],
        out_specs=new_spec,
    )
    return pl.pallas_call(
        functools.partial(_attn_sample_body, n_pages=n_pages, page=page),
        grid_spec=grid_spec,
        out_shape=jax.ShapeDtypeStruct((bs, SUBLANES, W_A), F32),
        compiler_params=_cparams("parallel"),
        name="attn_sample",
    )(page_table, bias, q8, k8, v8, *([cache_kt] * n_pages), *([cache_vt] * n_pages))


def _group_sum(x, ones_bd):
    nt = x.shape[1] // LANES
    stacked = jnp.concatenate([x[:, i * LANES:(i + 1) * LANES] for i in range(nt)], axis=0)
    s = _dot_exact_rhs(stacked, ones_bd)
    m = x.shape[0]
    return jnp.concatenate([s[i * m:(i + 1) * m] for i in range(nt)], axis=1)


def _rwkv_body(cols_ref, shift_ref, s0_ref, mu_ref, w0_ref, wwa_ref, a0_ref, g2_ref, kk_ref, ka_ref,
               rk_ref, lnw_ref, lnb_ref, ones_ref, ob_ref, sout_ref, s_ref, prev_ref, *, t_valid, padded):
    c = pl.program_id(1)
    nc = pl.num_programs(1)
    nb, ch, _ = cols_ref.shape
    m = nb * ch
    hps = HEADS_PER_SYSTEM
    grp = hps * D_B
    n_grp = W_B // grp
    sub = hps * ch

    @pl.when(c == 0)
    def _():
        prev_ref[...] = shift_ref[...]
        s_ref[...] = jnp.zeros_like(s_ref)
        for e in range(nb):
            for h in range(H_B):
                g, hh = divmod(h, hps)
                s_ref[e, g, hh * D_B:(hh + 1) * D_B, hh * D_B:(hh + 1) * D_B] = s0_ref[e, h]

    cols3 = cols_ref[...]
    cols = cols3.reshape(m, R_B)
    row = _iota2((m, 1), 0)
    first = jnp.broadcast_to(prev_ref[...], (nb, ch, R_B)).reshape(m, R_B)
    prev = jnp.where(row % ch == 0, first, pltpu.roll(cols, 1, 0))
    prev_ref[...] = cols3[:, ch - 1:ch, :]
    mixed = cols + mu_ref[...] * (prev - cols)
    r = mixed[:, 0:W_B]
    k = mixed[:, W_B:2 * W_B]
    v = mixed[:, 2 * W_B:3 * W_B]
    lora = mixed[:, 3 * W_B:3 * W_B + LORA_W + LORA_A]
    gl = mixed[:, 3 * W_B + LORA_W + LORA_A:]
    lora = jnp.where(_iota2(lora.shape, 1) < LORA_W, jnp.tanh(lora), lora)
    wa = _dot(lora, wwa_ref[...])
    w_log = -_softplus(-(w0_ref[...] + wa[:, :W_B])) - 0.5
    logw = -jnp.exp(w_log)
    a = _sigmoid(a0_ref[...] + wa[:, W_B:])
    g = _dot(_sigmoid(gl), g2_ref[...])
    ones_bd = ones_ref[...]
    kk = k * kk_ref[...]
    k2 = k * (1.0 + (a - 1.0) * ka_ref[...])
    sums = _group_sum(jnp.concatenate([kk * kk, r * k2 * rk_ref[...]], axis=0), ones_bd)
    kk = kk * lax.rsqrt(sums[0:m] + L2_EPS)
    bonus = sums[m:2 * m] * v
    v_s = v
    if padded:
        valid = (row % ch) + c * ch < t_valid
        logw = jnp.where(valid, logw, 0.0)
        kk = jnp.where(valid, kk, 0.0)
        k2 = jnp.where(valid, k2, 0.0)
        v_s = jnp.where(valid, v, 0.0)
    a_s = -kk
    b_s = kk * a

    cum = _dot_exact_lhs(_time_prefix_matrix(m, ch), logw)
    e_in = jnp.exp(cum)
    e_neg = jnp.exp(-cum)
    at = a_s * jnp.exp(cum - logw)
    bt = b_s * e_neg
    kt = k2 * e_neg
    rt = r * e_in

    nbs = SYSTEM_ROWS // sub
    n = nbs * sub
    bm = ((_iota2((n, grp), 0) // ch) % hps) == (_iota2((n, grp), 1) // D_B)
    pr = _iota2((n, n), 0)
    pc = _iota2((n, n), 1)
    same = (pr // ch) == (pc // ch)
    strict = same & (pc < pr)
    incl = same & (pc <= pr)
    n_levels = max(1, (ch - 1).bit_length())

    systems = [(range(e0, e0 + nbs), gi, slice(gi * grp, (gi + 1) * grp))
               for e0 in range(0, nb, nbs) for gi in range(n_grp)]
    per_seq = lambda x: [x[i * sub:(i + 1) * sub] for i in range(nbs)]
    cat = lambda xs: xs[0] if len(xs) == 1 else jnp.concatenate(xs, axis=0)

    def bd(x):
        out = []
        for seqs, _, sl in systems:
            pieces = [x[e * ch:(e + 1) * ch, sl] for e in seqs for _ in range(hps)]
            out.append(jnp.where(bm, jnp.concatenate(pieces, axis=0), 0.0).astype(BF16))
        return out

    ab, bb, kb, rb, vb = (bd(x) for x in (at, bt, kt, rt, v_s))
    ar = [jnp.concatenate([a_, r_], axis=0) for a_, r_ in zip(ab, rb)]
    s_old = [[s_ref[e, gi] for e in seqs] for seqs, gi, _ in systems]
    s16 = [[s.astype(BF16) for s in ss] for ss in s_old]
    ar_b = [_dot_nt(x, y_) for x, y_ in zip(ar, bb)]
    ar_k = [_dot_nt(x, y_) for x, y_ in zip(ar, kb)]
    l_ab = [jnp.where(strict, x[0:n], 0.0) for x in ar_b]
    m_rb = [jnp.where(incl, x[n:2 * n], 0.0).astype(BF16) for x in ar_b]
    l_ak = [jnp.where(strict, x[0:n], 0.0) for x in ar_k]
    m_rk = [jnp.where(incl, x[n:2 * n], 0.0) for x in ar_k]
    a_s0 = [cat([_dot_nt(x, s) for x, s in zip(per_seq(a_), ss)]) for a_, ss in zip(ab, s16)]
    r_s0 = [cat([_dot_nt(x, s) for x, s in zip(per_seq(r_), ss)]) for r_, ss in zip(rb, s16)]
    rhs = [x + _dot(l, v_) for x, l, v_ in zip(a_s0, l_ak, vb)]
    y_v = [x + _dot(mk, v_) for x, mk, v_ in zip(r_s0, m_rk, vb)]
    eye = jnp.where(pr == pc, 1.0, 0.0)
    inv = [eye + l for l in l_ab]
    pw = l_ab
    for _ in range(n_levels - 1):
        pw = [_dot(p_, p_) for p_ in pw]
        inv = [i_ + _dot(i_, p_) for i_, p_ in zip(inv, pw)]
    ub = [_dot(i_, x).astype(BF16) for i_, x in zip(inv, rhs)]
    y_parts = {}
    for si, (seqs, gi, sl) in enumerate(systems):
        y = y_v[si] + jnp.dot(m_rb[si], ub[si], preferred_element_type=F32)
        for e, ye in zip(seqs, per_seq(y)):
            y_parts[e, gi] = sum(ye[h * ch:(h + 1) * ch] for h in range(hps))
        parts = zip(seqs, per_seq(ub[si]), per_seq(bb[si]), per_seq(vb[si]), per_seq(kb[si]), s_old[si])
        for e, ue, be, ve, ke, so in parts:
            wc = e_in[e * ch + ch - 1:e * ch + ch, sl]
            s_ref[e, gi] = (so + _dot_tn(ue, be) + _dot_tn(ve, ke)) * wc
    y = cat([jnp.concatenate([y_parts[e, gi] for gi in range(n_grp)], axis=1) for e in range(nb)])

    inv_d = 1.0 / D_B
    dev = y - _group_sum(y, ones_bd) * inv_d
    var = _group_sum(dev * dev, ones_bd) * inv_d
    yn = dev * lax.rsqrt(var + LN_X_EPS) * lnw_ref[...] + lnb_ref[...]
    ob_ref[...] = ((yn + bonus) * g).reshape(nb, ch, W_B)

    @pl.when(c == nc - 1)
    def _():
        for e in range(nb):
            for h in range(H_B):
                gq, hh = divmod(h, hps)
                sout_ref[e, h] = s_ref[e, gq, hh * D_B:(hh + 1) * D_B, hh * D_B:(hh + 1) * D_B]


def _seqs_per_step(b, ch, heads, wanted_systems):
    nbs = SYSTEM_ROWS // (heads * ch)
    nb = nbs * wanted_systems
    while nb > nbs and b % nb:
        nb -= nbs
    assert b % nb == 0
    return nb


def _rwkv(cols, shift, s0, prm, l, t_valid, ch):
    b, t, _ = cols.shape
    nb = _seqs_per_step(b, ch, HEADS_PER_SYSTEM, 2 if t > ch else 1)
    nc = t // ch
    assert t % ch == 0 and (ch == PROMPT_CHUNK or nc == 1)
    vec = lambda n: pl.BlockSpec((None, 1, n), lambda bi, c: (l, 0, 0))
    mat = lambda k, n: pl.BlockSpec((None, k, n), lambda bi, c: (l, 0, 0))
    return pl.pallas_call(
        functools.partial(_rwkv_body, t_valid=t_valid, padded=t_valid < t),
        grid=(b // nb, nc),
        in_specs=[
            pl.BlockSpec((nb, ch, R_B), lambda bi, c: (bi, c, 0)),
            pl.BlockSpec((nb, 1, R_B), lambda bi, c: (bi, 0, 0)),
            pl.BlockSpec((nb, H_B, D_B, D_B), lambda bi, c: (bi, 0, 0, 0)),
            vec(R_B), vec(W_B), mat(LORA_W + LORA_A, 2 * W_B), vec(W_B), mat(LORA_G, W_B),
            vec(W_B), vec(W_B), vec(W_B), vec(W_B), vec(W_B),
            pl.BlockSpec((LANES, LANES), lambda bi, c: (0, 0)),
        ],
        out_specs=[
            pl.BlockSpec((nb, ch, W_B), lambda bi, c: (bi, c, 0)),
            pl.BlockSpec((nb, H_B, D_B, D_B), lambda bi, c: (bi, 0, 0, 0)),
        ],
        out_shape=[jax.ShapeDtypeStruct((b, t, W_B), F32),
                   jax.ShapeDtypeStruct((b, H_B, D_B, D_B), F32)],
        scratch_shapes=[pltpu.VMEM((nb, W_B // (HEADS_PER_SYSTEM * D_B), HEADS_PER_SYSTEM * D_B,
                                    HEADS_PER_SYSTEM * D_B), F32),
                        pltpu.VMEM((nb, 1, R_B), F32)],
        compiler_params=_cparams("parallel", "arbitrary"),
        name="rwkv7",
    )(cols, shift, s0, prm["mu"], prm["w0"], prm["wwa"], prm["a0"], prm["g2"], prm["k_k"], prm["k_a"],
      prm["r_k"], prm["ln_w"], prm["ln_b"], prm["ones_bd"])


def _gdn_body(x_ref, ab_ref, z_ref, cprev_ref, s0_ref, cw_ref, alog_ref, dtb_ref, nw_ref,
              oc_ref, sout_ref, s_ref, xbuf_ref, *, t_valid, padded):
    c = pl.program_id(1)
    nc = pl.num_programs(1)
    nb, ch, _ = x_ref.shape
    m = nb * ch
    hist = SUBLANES

    @pl.when(c == 0)
    def _():
        xbuf_ref[:, 0:hist, :] = cprev_ref[...]
        s_ref[...] = s0_ref[...]

    x3 = x_ref[...]
    xbuf_ref[:, hist:hist + ch, :] = x3
    cw = cw_ref[...]
    conv = x3.reshape(m, 3 * W_C) * cw[CONV_W - 1:CONV_W, :]
    for i in range(1, CONV_W):
        conv = conv + xbuf_ref[:, hist - i:hist - i + ch, :].reshape(m, 3 * W_C) * cw[CONV_W - 1 - i:CONV_W - i, :]
    conv = conv * _sigmoid(conv)
    xbuf_ref[:, 0:hist, :] = x3[:, ch - hist:ch, :]

    def stack(y):
        return jnp.concatenate([y[e * ch:(e + 1) * ch, h * D_C:(h + 1) * D_C]
                                for e in range(nb) for h in range(H_C)], axis=0)

    q = stack(conv[:, 0:W_C])
    k = stack(conv[:, W_C:2 * W_C])
    v = stack(conv[:, 2 * W_C:3 * W_C])
    q = q * lax.rsqrt(jnp.sum(q * q, axis=-1, keepdims=True) + L2_EPS) * (D_C ** -0.5)
    k = k * lax.rsqrt(jnp.sum(k * k, axis=-1, keepdims=True) + L2_EPS)

    ab = ab_ref[...].reshape(m, LANES)
    g_all = -jnp.exp(alog_ref[...]) * _softplus(ab + dtb_ref[...])
    beta_all = _sigmoid(ab)
    if padded:
        valid = (_iota2((m, 1), 0) % ch) + c * ch < t_valid
        g_all = jnp.where(valid, g_all, 0.0)
        beta_all = jnp.where(valid, beta_all, 0.0)
    cum_all = _dot_exact_lhs(_time_prefix_matrix(m, ch), g_all)

    def rows(src, lane0, last_only=False):
        out = []
        for e in range(nb):
            for h in range(H_C):
                lo = e * ch + (ch - 1 if last_only else 0)
                piece = src[lo:(e + 1) * ch, lane0 + h:lane0 + h + 1]
                out.append(jnp.broadcast_to(piece, (ch, D_C)))
        return jnp.concatenate(out, axis=0)

    cum = rows(cum_all, 0)
    beta = rows(beta_all, H_C)
    g_last = rows(cum_all, 0, last_only=True)

    n = SYSTEM_ROWS
    n_sys = nb * H_C * ch // n
    per_sys = lambda y: [y[i * n:(i + 1) * n] for i in range(n_sys)]
    pr = _iota2((n, n), 0)
    pc = _iota2((n, n), 1)
    same = (pr // ch) == (pc // ch)
    strict = same & (pc < pr)
    incl = same & (pc <= pr)
    decay = []
    for cum_s in per_sys(cum):
        cum_col = jnp.concatenate([cum_s] * (n // D_C), axis=1)
        cum_row = jnp.broadcast_to(jnp.transpose(cum_s)[0:1, :], (n, n))
        decay.append(jnp.where(incl, jnp.exp(jnp.where(incl, cum_col - cum_row, 0.0)), 0.0))

    kbeta = k * beta
    kb16 = k.astype(BF16)
    e_cum = jnp.exp(cum)
    kk_t = [_dot_nt(x, y) for x, y in zip(per_sys(kbeta), per_sys(kb16))]
    qk_t = [_dot_nt(x, y) for x, y in zip(per_sys(q), per_sys(kb16))]
    neg_a = [jnp.where(strict, -(x * d), 0.0) for x, d in zip(kk_t, decay)]
    eye = jnp.where(pr == pc, 1.0, 0.0)
    inv = [eye + l_ for l_ in neg_a]
    pw = neg_a
    for _ in range(max(1, (ch - 1).bit_length()) - 1):
        pw = [_dot(p_, p_) for p_ in pw]
        inv = [i_ + _dot(i_, p_) for i_, p_ in zip(inv, pw)]
    vk = jnp.concatenate([v * beta, kbeta * e_cum], axis=1)
    uw = jnp.concatenate([_dot(i_, x) for i_, x in zip(inv, per_sys(vk))], axis=0)
    u_cap = uw[:, :D_C]
    w_cap = uw[:, D_C:]
    qk = [x * d for x, d in zip(qk_t, decay)]
    q_dec = q * e_cum
    k_dec = k * jnp.exp(g_last - cum)

    heads = [(e, h, slice((e * H_C + h) * ch, (e * H_C + h + 1) * ch)) for e in range(nb) for h in range(H_C)]
    s_old = [s_ref[e, h] for e, h, _ in heads]
    ws = [_dot(jnp.concatenate([w_cap[sl], q_dec[sl]], axis=0), s) for (_, _, sl), s in zip(heads, s_old)]
    v_new = [u_cap[sl] - w[0:ch] for (_, _, sl), w in zip(heads, ws)]
    o_state = [w[ch:2 * ch] for w in ws]
    for (e, h, sl), s, vn in zip(heads, s_old, v_new):
        s_ref[e, h] = s * jnp.exp(g_last[sl.start:sl.start + 1, :]) + _dot_tn(k_dec[sl], vn)
    vn_all = jnp.concatenate(v_new, axis=0)
    o = jnp.concatenate(o_state, axis=0) + jnp.concatenate(
        [_dot(x, y) for x, y in zip(qk, per_sys(vn_all))], axis=0)

    z3 = z_ref[...]
    for e in range(nb):
        for h in range(H_C):
            i = e * H_C + h
            oh = o[i * ch:(i + 1) * ch]
            oh = oh * lax.rsqrt(jnp.mean(oh * oh, axis=-1, keepdims=True) + NORM_EPS) * nw_ref[...]
            zh = z3[e, :, h * D_C:(h + 1) * D_C]
            oc_ref[e, :, h * D_C:(h + 1) * D_C] = oh * (zh * _sigmoid(zh))

    @pl.when(c == nc - 1)
    def _():
        sout_ref[...] = s_ref[...]


def _gdn(x, ab, z, cprev, s0, prm, l, t_valid, ch):
    b, t, _ = x.shape
    nb = _seqs_per_step(b, ch, H_C, 4 if t > ch else 1)
    nc = t // ch
    assert t % ch == 0 and ch >= SUBLANES
    vec = lambda n: pl.BlockSpec((None, 1, n), lambda bi, c: (l, 0, 0))
    return pl.pallas_call(
        functools.partial(_gdn_body, t_valid=t_valid, padded=t_valid < t),
        grid=(b // nb, nc),
        in_specs=[
            pl.BlockSpec((nb, ch, 3 * W_C), lambda bi, c: (bi, c, 0)),
            pl.BlockSpec((nb, ch, LANES), lambda bi, c: (bi, c, 0)),
            pl.BlockSpec((nb, ch, W_C), lambda bi, c: (bi, c, 0)),
            pl.BlockSpec((nb, SUBLANES, 3 * W_C), lambda bi, c: (bi, 0, 0)),
            pl.BlockSpec((nb, H_C, D_C, D_C), lambda bi, c: (bi, 0, 0, 0)),
            pl.BlockSpec((None, CONV_W, 3 * W_C), lambda bi, c: (l, 0, 0)),
            vec(LANES), vec(LANES), vec(D_C),
        ],
        out_specs=[
            pl.BlockSpec((nb, ch, W_C), lambda bi, c: (bi, c, 0)),
            pl.BlockSpec((nb, H_C, D_C, D_C), lambda bi, c: (bi, 0, 0, 0)),
        ],
        out_shape=[jax.ShapeDtypeStruct((b, t, W_C), F32),
                   jax.ShapeDtypeStruct((b, H_C, D_C, D_C), F32)],
        scratch_shapes=[pltpu.VMEM((nb, H_C, D_C, D_C), F32),
                        pltpu.VMEM((nb, SUBLANES + ch, 3 * W_C), F32)],
        compiler_params=_cparams("parallel", "arbitrary"),
        name="gdn",
    )(x, ab, z, cprev, s0, prm["conv_w"], prm["a_log"], prm["dt_bias"], prm["norm_w"])


def _pad_time(x, t_to):
    return jnp.pad(x, ((0, 0), (0, t_to - x.shape[1]), (0, 0)))


def kernel(x_prompt, x_sample, cache_k, cache_v, page_table, state_rwkv, state_rwkv_shift, state_gdn,
           state_gdn_conv, ffn1_norm, ffn1_w_gate, ffn1_w_up, ffn1_w_down, mix_norm, w_in, sb_bias,
           rwkv_mu, rwkv_w0, rwkv_w2, rwkv_a0, rwkv_a2, rwkv_g2, rwkv_k_k, rwkv_k_a, rwkv_r_k,
           rwkv_ln_w, rwkv_ln_b, gdn_conv_w, gdn_a_log, gdn_dt_bias, gdn_norm_w,
           w_br_a, w_br_b, w_br_c, w_out, ffn2_norm, ffn2_w_gate, ffn2_w_up, ffn2_w_down, final_norm):
    bp, t, d = x_prompt.shape
    bs, ts, _ = x_sample.shape
    n_layers = ffn1_norm.shape[0]
    n_pool, _, page, _, _ = cache_k.shape
    assert t % PROMPT_CHUNK == 0 and ts <= SUBLANES

    bf = lambda w: w.astype(BF16)
    row3 = lambda w: w.reshape(n_layers, 1, -1)
    f1g, f1u, f1d = bf(ffn1_w_gate), bf(ffn1_w_up), bf(ffn1_w_down)
    f2g, f2u, f2d = bf(ffn2_w_gate), bf(ffn2_w_up), bf(ffn2_w_down)
    o_b = 3 * W_A
    o_c = o_b + R_B
    o_ab = o_c + 3 * W_C
    o_z = o_ab + 2 * H_C
    w1 = bf(w_in[:, :, :o_c])
    w2 = bf(jnp.concatenate([
        w_in[:, :, o_c:o_ab],
        jnp.pad(w_in[:, :, o_ab:o_z], ((0, 0), (0, 0), (0, LANES - 2 * H_C))),
        w_in[:, :, o_z:], ], axis=-1))
    widths1 = (W_A, W_A, W_A, R_B)
    widths2 = (3 * W_C, LANES, W_C, 3 * d)
    wba, wbb, wbc, wo = bf(w_br_a), bf(w_br_b), bf(w_br_c), bf(w_out)
    zeros_wa = jnp.zeros((n_layers, LORA_W, W_B), F32)
    wwa = bf(jnp.concatenate([
        jnp.concatenate([rwkv_w2, zeros_wa], axis=-1),
        jnp.concatenate([zeros_wa, rwkv_a2], axis=-1)], axis=1))
    gi = jnp.arange(LANES) // D_B
    rwkv_prm = dict(
        mu=row3(rwkv_mu), w0=row3(rwkv_w0), wwa=wwa, a0=row3(rwkv_a0), g2=bf(rwkv_g2),
        k_k=row3(rwkv_k_k), k_a=row3(rwkv_k_a), r_k=row3(rwkv_r_k), ln_w=row3(rwkv_ln_w),
        ln_b=row3(rwkv_ln_b), ones_bd=(gi[:, None] == gi[None, :]).astype(BF16))
    lane_pad = lambda w: jnp.pad(w, ((0, 0), (0, LANES - w.shape[1]))).reshape(n_layers, 1, LANES)
    gdn_prm = dict(conv_w=gdn_conv_w, a_log=lane_pad(gdn_a_log), dt_bias=lane_pad(gdn_dt_bias),
                   norm_w=row3(gdn_norm_w))
    n1a, n1b, n2 = row3(ffn1_norm), row3(mix_norm), row3(ffn2_norm)
    cache_kt = jnp.transpose(cache_k, (0, 1, 3, 4, 2)).reshape(n_pool, n_layers, W_A, page)
    cache_vt = jnp.transpose(cache_v, (0, 1, 3, 4, 2)).reshape(n_pool, n_layers, W_A, page)

    xp = x_prompt.reshape(bp * t, d)
    xs = x_sample.reshape(bs * ts, d)
    zero_shift = jnp.zeros((bp, 1, R_B), F32)
    zero_rwkv = jnp.zeros((bp, H_B, D_B, D_B), F32)
    zero_conv = jnp.zeros((bp, SUBLANES, 3 * W_C), F32)
    zero_gdn = jnp.zeros((bp, H_C, D_C, D_C), F32)
    pad8 = lambda y: _pad_time(y.reshape(bs, ts, -1), SUBLANES)

    outs = [[] for _ in range(12)]
    for l in range(n_layers):
        xp = _ffn(xp, n1a, f1g, f1u, f1d, l)
        xs = _ffn(xs, n1a, f1g, f1u, f1d, l)
        qp, kp, vp, bcp = _proj(xp, n1b, w1, l, widths1, TM_DENSE)
        cxp, cabp, czp, gtp = _proj(xp, n1b, w2, l, widths2, TM_WIDE_PROJ, sigmoid_last=True)
        qs, ks, vs, bcs = _proj(xs, n1b, w1, l, widths1, TM_DENSE)
        cxs, cabs, czs, gts = _proj(xs, n1b, w2, l, widths2, TM_WIDE_PROJ, sigmoid_last=True)

        oap = _attn_prompt(qp, kp, vp, sb_bias[l], bp, t)
        oas = _attn_sample(pad8(qs), pad8(ks), pad8(vs), cache_kt, cache_vt, page_table, sb_bias[l], l)
        oas = oas[:, :ts].reshape(bs * ts, W_A)

        bcp3 = bcp.reshape(bp, t, R_B)
        obp, rwp = _rwkv(bcp3, zero_shift, zero_rwkv, rwkv_prm, l, t, PROMPT_CHUNK)
        bcs3 = bcs.reshape(bs, ts, R_B)
        obs, rws = _rwkv(pad8(bcs), state_rwkv_shift[l][:, None, :], state_rwkv[l], rwkv_prm, l, ts, SUBLANES)
        obs = obs[:, :ts].reshape(bs * ts, W_B)

        cxp3 = cxp.reshape(bp, t, 3 * W_C)
        ocp, gdp = _gdn(cxp3, cabp.reshape(bp, t, LANES), czp.reshape(bp, t, W_C), zero_conv, zero_gdn,
                        gdn_prm, l, t, PROMPT_CHUNK)
        cxs3 = cxs.reshape(bs, ts, 3 * W_C)
        conv_full = jnp.concatenate([state_gdn_conv[l], cxs3], axis=1)
        cprev_s = jnp.pad(state_gdn_conv[l], ((0, 0), (SUBLANES - (CONV_W - 1), 0), (0, 0)))
        ocs, gds = _gdn(pad8(cxs), pad8(cabs), pad8(czs), cprev_s, state_gdn[l], gdn_prm, l, ts, SUBLANES)
        ocs = ocs[:, :ts].reshape(bs * ts, W_C)

        xp = _merge(xp, oap, obp.reshape(bp * t, W_B), ocp.reshape(bp * t, W_C), gtp, wba, wbb, wbc, wo, l)
        xs = _merge(xs, oas, obs, ocs, gts, wba, wbb, wbc, wo, l)
        xp = _ffn(xp, n2, f2g, f2u, f2d, l)
        xs = _ffn(xs, n2, f2g, f2u, f2d, l)

        per_layer = (kp.reshape(bp, t, H_A, D_A), vp.reshape(bp, t, H_A, D_A),
                     ks.reshape(bs, ts, H_A, D_A), vs.reshape(bs, ts, H_A, D_A),
                     rwp, rws, bcp3[:, t - 1], bcs3[:, ts - 1], gdp, gds,
                     cxp3[:, t - (CONV_W - 1):], conv_full[:, ts:])
        for acc, val in zip(outs, per_layer):
            acc.append(val)

    y_prompt = _final_norm(xp, final_norm.reshape(1, d)).reshape(bp, t, d)
    y_sample = _final_norm(xs, final_norm.reshape(1, d)).reshape(bs, ts, d)
    stack_axes = (1, 1, 1, 1, 0, 0, 0, 0, 0, 0, 0, 0)
    return (y_prompt, y_sample) + tuple(jnp.stack(v, axis=ax) for v, ax in zip(outs, stack_axes))
```

```python
import functools

import jax
import jax.numpy as jnp
from jax import lax
from jax.experimental import pallas as pl
from jax.experimental.pallas import tpu as pltpu

F32 = jnp.float32
BF16 = jnp.bfloat16

NORM_EPS = 1e-6
L2_EPS = 1e-6
LOG2_E = 1.4426950408889634
H_A, D_A = 8, 64
W_A = H_A * D_A
H_B, D_B = 8, 64
W_B = H_B * D_B
LORA_W, LORA_A, LORA_G = 64, 64, 128
R_B = 3 * W_B + LORA_W + LORA_A + LORA_G
LN_X_EPS = D_B * 1e-5
H_C, D_C = 4, 128
W_C = H_C * D_C
CONV_W = 4
LANES = 128
SUBLANES = 8
HEADS_PER_SYSTEM = 4
SYSTEM_ROWS = 256
PROMPT_CHUNK = SYSTEM_ROWS // HEADS_PER_SYSTEM
TM_DENSE = 512
TM_WIDE_PROJ = 512
TQ_ATTN = 256
ATTN_PIPELINE_LAG = 2
SAMPLE_PIPELINE_LAG = 4
VMEM_LIMIT_BYTES = 52 * 1024 * 1024


def _cparams(*sem):
    return pltpu.CompilerParams(dimension_semantics=sem, vmem_limit_bytes=VMEM_LIMIT_BYTES)


def _dot(a, b):
    return jnp.dot(a.astype(BF16), b.astype(BF16), preferred_element_type=F32)


def _dot_nt(a, b):
    return lax.dot_general(a.astype(BF16), b.astype(BF16), (((1,), (1,)), ((), ())),
                           preferred_element_type=F32)


def _dot_tn(a, b):
    return lax.dot_general(a.astype(BF16), b.astype(BF16), (((0,), (0,)), ((), ())),
                           preferred_element_type=F32)


def _split(x):
    hi = x.astype(BF16)
    lo = (x - hi.astype(F32)).astype(BF16)
    return hi, lo


def _dot_exact_rhs(a, b_bf16):
    ah, al = _split(a)
    return (jnp.dot(ah, b_bf16, preferred_element_type=F32)
            + jnp.dot(al, b_bf16, preferred_element_type=F32))


def _dot_exact_lhs(a_bf16, b):
    bh, bl = _split(b)
    return (jnp.dot(a_bf16, bh, preferred_element_type=F32)
            + jnp.dot(a_bf16, bl, preferred_element_type=F32))


def _softplus(z):
    return jnp.maximum(z, 0.0) + jnp.log(1.0 + jnp.exp(-jnp.abs(z)))


def _softplus_log2(zl):
    sign = jnp.uint32(0x80000000)
    neg_abs = lax.bitcast_convert_type(lax.bitcast_convert_type(zl, jnp.uint32) | sign, F32)
    return jnp.maximum(zl, 0.0) + jnp.log2(1.0 + jnp.exp2(neg_abs))


def _sigmoid(z):
    return 1.0 / (1.0 + jnp.exp(-z))


def _rms(x, w):
    return x * lax.rsqrt(jnp.mean(x * x, axis=-1, keepdims=True) + NORM_EPS) * w


def _iota2(shape, dim):
    return lax.broadcasted_iota(jnp.int32, shape, dim)


def _time_prefix_matrix(m, ch):
    r = _iota2((m, m), 0)
    c = _iota2((m, m), 1)
    return jnp.where(((r // ch) == (c // ch)) & (c <= r), 1.0, 0.0).astype(BF16)


def _ffn_body(x_ref, nw_ref, wg_ref, wu_ref, wd_ref, o_ref):
    x = x_ref[...]
    h = _rms(x, nw_ref[...]).astype(BF16)
    g = jnp.dot(h, wg_ref[...], preferred_element_type=F32)
    u = jnp.dot(h, wu_ref[...], preferred_element_type=F32)
    act = (g * _sigmoid(g) * u).astype(BF16)
    o_ref[...] = x + 0.5 * jnp.dot(act, wd_ref[...], preferred_element_type=F32)


def _ffn(x, nw, wg, wu, wd, l):
    m, d = x.shape
    ff = wg.shape[-1]
    tm = min(TM_DENSE, m)
    once = pl.Buffered(1)
    return pl.pallas_call(
        _ffn_body,
        grid=(m // tm,),
        in_specs=[
            pl.BlockSpec((tm, d), lambda i: (i, 0)),
            pl.BlockSpec((None, 1, d), lambda i: (l, 0, 0)),
            pl.BlockSpec((None, d, ff), lambda i: (l, 0, 0), pipeline_mode=once),
            pl.BlockSpec((None, d, ff), lambda i: (l, 0, 0), pipeline_mode=once),
            pl.BlockSpec((None, ff, d), lambda i: (l, 0, 0), pipeline_mode=once),
        ],
        out_specs=pl.BlockSpec((tm, d), lambda i: (i, 0)),
        out_shape=jax.ShapeDtypeStruct((m, d), F32),
        compiler_params=_cparams("parallel"),
        name="ffn",
    )(x, nw, wg, wu, wd)


def _proj_body(x_ref, nw_ref, w_ref, *refs, widths, sigmoid_last, transposed):
    o_refs = refs[len(refs) - len(widths):]
    h = _rms(x_ref[...], nw_ref[...]).astype(BF16)
    off = 0
    for idx, (o_ref, wd) in enumerate(zip(o_refs, widths)):
        y = jnp.dot(h, w_ref[:, off:off + wd], preferred_element_type=F32)
        if sigmoid_last and idx == len(widths) - 1:
            y = _sigmoid(y)
        o_ref[...] = y.T if idx in transposed else y
        off += wd


def _proj(x, nw, w, l, widths, tm, sigmoid_last=False, transposed=(), stacks=None, batch_time=None):
    m, d = x.shape
    n = w.shape[-1]
    assert sum(widths) == n
    tm = min(tm, m)
    out_specs, out_shape, aliases, extra = [], [], {}, []
    for idx, wd in enumerate(widths):
        if idx in transposed:
            b, t = batch_time
            nt = t // tm
            out_specs.append(pl.BlockSpec((None, None, wd, tm), lambda i, nt=nt: (i // nt, l, 0, i % nt)))
            out_shape.append(jax.ShapeDtypeStruct((b, w.shape[0], wd, t), F32))
            if stacks is not None:
                aliases[3 + len(extra)] = idx
                extra.append(stacks[transposed.index(idx)])
        else:
            out_specs.append(pl.BlockSpec((tm, wd), lambda i: (i, 0)))
            out_shape.append(jax.ShapeDtypeStruct((m, wd), F32))
    return pl.pallas_call(
        functools.partial(_proj_body, widths=widths, sigmoid_last=sigmoid_last, transposed=transposed),
        grid=(m // tm,),
        in_specs=[
            pl.BlockSpec((tm, d), lambda i: (i, 0)),
            pl.BlockSpec((None, 1, d), lambda i: (l, 0, 0)),
            pl.BlockSpec((None, d, n), lambda i: (l, 0, 0), pipeline_mode=pl.Buffered(1)),
        ] + [pl.BlockSpec(memory_space=pl.ANY)] * len(extra),
        out_specs=out_specs,
        out_shape=out_shape,
        input_output_aliases=aliases,
        compiler_params=_cparams("parallel"),
        name="proj",
    )(x, nw, w, *extra)


def _merge_body(x_ref, oa_ref, ob_ref, oc_ref, ga_ref, gb_ref, gc_ref,
                wa_ref, wb_ref, wc_ref, wo_ref, o_ref):
    merged = (ga_ref[...] * _dot(oa_ref[...], wa_ref[...])
              + gb_ref[...] * _dot(ob_ref[...], wb_ref[...])
              + gc_ref[...] * _dot(oc_ref[...], wc_ref[...]))
    o_ref[...] = x_ref[...] + _dot(merged, wo_ref[...])


def _merge(x, oa, ob, oc, gates, wa, wb, wc, wo, l):
    m, d = x.shape
    tm = min(TM_DENSE, m)
    row = lambda i: (i, 0)
    wspec = lambda k: pl.BlockSpec((None, k, d), lambda i: (l, 0, 0))
    return pl.pallas_call(
        _merge_body,
        grid=(m // tm,),
        in_specs=[
            pl.BlockSpec((tm, d), row),
            pl.BlockSpec((tm, W_A), row), pl.BlockSpec((tm, W_B), row), pl.BlockSpec((tm, W_C), row),
            pl.BlockSpec((tm, d), lambda i: (i, 0)),
            pl.BlockSpec((tm, d), lambda i: (i, 1)),
            pl.BlockSpec((tm, d), lambda i: (i, 2)),
            wspec(W_A), wspec(W_B), wspec(W_C), wspec(d),
        ],
        out_specs=pl.BlockSpec((tm, d), row),
        out_shape=jax.ShapeDtypeStruct((m, d), F32),
        compiler_params=_cparams("parallel"),
        name="merge",
    )(x, oa, ob, oc, gates, gates, gates, wa, wb, wc, wo)


def _norm_body(x_ref, w_ref, o_ref):
    o_ref[...] = _rms(x_ref[...], w_ref[...])


def _final_norm(x, w):
    m, d = x.shape
    tm = min(TM_DENSE, m)
    return pl.pallas_call(
        _norm_body,
        grid=(m // tm,),
        in_specs=[pl.BlockSpec((tm, d), lambda i: (i, 0)), pl.BlockSpec((1, d), lambda i: (0, 0))],
        out_specs=pl.BlockSpec((tm, d), lambda i: (i, 0)),
        out_shape=jax.ShapeDtypeStruct((m, d), F32),
        compiler_params=_cparams("parallel"),
        name="final_norm",
    )(x, w)


def _sb_sweep(chains, tri, lag):
    order = [(c, i) for i in range(max(len(ch[0]) for ch in chains))
             for c, ch in enumerate(chains) if i < len(ch[0])]
    state = [[o, r] for _, _, _, o, r in chains]
    z, cs = {}, {}
    for step in range(len(order) + 2 * lag):
        if step < len(order):
            c, i = order[step]
            z[c, i] = chains[c][0][i]()
        if 0 <= step - lag < len(order):
            c, i = order[step - lag]
            sp = _softplus_log2(z[c, i])
            if chains[c][1][i] is not None:
                sp = jnp.where(chains[c][1][i], sp, 0.0)
            cs[c, i] = jnp.dot(sp.astype(BF16), tri, preferred_element_type=F32)
        if 0 <= step - 2 * lag < len(order):
            c, i = order[step - 2 * lag]
            st = state[c]
            a = jnp.exp2(z.pop((c, i)) - (st[1] + cs[c, i]))
            if chains[c][1][i] is not None:
                a = jnp.where(chains[c][1][i], a, 0.0)
            st[0] = st[0] + chains[c][2][i](a.astype(BF16))
            st[1] = st[1] + cs.pop((c, i))[:, 0:1]
    return tuple((o, r) for o, r in state)


def _suffix_matrix(tk):
    return jnp.where(_iota2((tk, tk), 0) >= _iota2((tk, tk), 1), 1.0, 0.0).astype(BF16)


def _attn_prompt_body(bias_ref, q_ref, k_ref, v_ref, o_ref, kb_ref, vb_ref, *, tq):
    p = pl.program_id(1)
    qi = pl.program_id(2)
    heads = LANES // D_A

    @pl.when(qi == 0)
    def _():
        kb_ref[...] = k_ref[...].astype(BF16)
        vb_ref[...] = v_ref[...].astype(BF16)

    subs = 2
    lane = _iota2((tq, LANES), 1)
    causal = _iota2((tq, tq), 1) < _iota2((tq, tq), 0)
    tri = _suffix_matrix(tq)
    biases = [bias_ref[heads * p + hh] * LOG2_E for hh in range(heads)]
    qhs = []
    for s in range(subs):
        q = q_ref[s * tq:(s + 1) * tq, :] * (D_A ** -0.5 * LOG2_E)
        qhs.append([jnp.where(lane // D_A == hh, q, 0.0).astype(BF16) for hh in range(heads)])

    def sweep(js, plan, carry):
        kv = []
        for j in js:
            start = pl.multiple_of(j * tq, tq)
            kv.append((kb_ref[:, pl.ds(start, tq)], vb_ref[:, pl.ds(start, tq)]))
        chains = []
        for s in range(subs):
            for hh in range(heads):
                zs = [lambda qh=qhs[s][hh], kblk=kv[i][0], b=biases[hh]:
                      jnp.dot(qh, kblk, preferred_element_type=F32) + b for i, _ in plan[s]]
                pvs = [lambda a, vblk=kv[i][1]: _dot_nt(a, vblk) for i, _ in plan[s]]
                chains.append((zs, [m for _, m in plan[s]], pvs) + tuple(carry[s * heads + hh]))
        return _sb_sweep(chains, tri, ATTN_PIPELINE_LAG)

    init = tuple((jnp.zeros((tq, LANES), F32), jnp.zeros((tq, 1), F32)) for _ in range(subs * heads))
    j_lo = subs * qi
    carry = sweep([j_lo + 1, j_lo], [[(1, causal)], [(0, causal), (1, None)]], init)

    def pair(jj, c):
        j = j_lo - 1 - 2 * jj
        return sweep([j, j - 1], [[(0, None), (1, None)]] * subs, c)

    carry = lax.fori_loop(0, qi, pair, carry)
    for s in range(subs):
        o_ref[s * tq:(s + 1) * tq, :] = jnp.where(lane < D_A, carry[s * heads][0], carry[s * heads + 1][0])


def _attn_prompt(q, kt, vt, bias, l, b, t):
    m = b * t
    tq = TQ_ATTN
    rows = 2 * tq
    assert t % rows == 0
    nq = t // rows
    npair = W_A // LANES
    kv_spec = pl.BlockSpec((None, None, LANES, t), lambda bi, p, qi, bias: (bi, l, p, 0))
    grid_spec = pltpu.PrefetchScalarGridSpec(
        num_scalar_prefetch=1,
        grid=(b, npair, nq),
        in_specs=[
            pl.BlockSpec((rows, LANES), lambda bi, p, qi, bias: (bi * nq + qi, p)),
            kv_spec, kv_spec,
        ],
        out_specs=pl.BlockSpec((rows, LANES), lambda bi, p, qi, bias: (bi * nq + qi, p)),
        scratch_shapes=[pltpu.VMEM((LANES, t), BF16), pltpu.VMEM((LANES, t), BF16)],
    )
    return pl.pallas_call(
        functools.partial(_attn_prompt_body, tq=tq),
        grid_spec=grid_spec,
        out_shape=jax.ShapeDtypeStruct((m, W_A), F32),
        compiler_params=_cparams("parallel", "parallel", "arbitrary"),
        name="attn_prompt",
    )(bias, q, kt, vt)


def _attn_sample_body(pt_ref, bias_ref, q_ref, kn_ref, vn_ref, *rest, n_pages, page):
    kp_refs = rest[:n_pages]
    vp_refs = rest[n_pages:2 * n_pages]
    o_ref = rest[2 * n_pages]
    nrow = H_A * SUBLANES
    q8 = q_ref[...] * (D_A ** -0.5 * LOG2_E)
    head_of_lane = _iota2((SUBLANES, W_A), 1) // D_A
    qm = jnp.concatenate([jnp.where(head_of_lane == h, q8, 0.0) for h in range(H_A)], axis=0).astype(BF16)
    ri = _iota2((nrow, page), 0)
    bias = jnp.zeros((nrow, page), F32)
    for h in range(H_A):
        bias = jnp.where(ri // SUBLANES == h, bias_ref[h] * LOG2_E, bias)
    mask_new = _iota2((nrow, page), 1) < (ri % SUBLANES)
    tri = _suffix_matrix(page)
    pad = jnp.zeros((page - SUBLANES, W_A), F32)
    k_new = jnp.concatenate([kn_ref[...], pad], axis=0).astype(BF16)
    v_new = jnp.concatenate([vn_ref[...], pad], axis=0).astype(BF16)
    zs = [lambda: _dot_nt(qm, k_new) + bias]
    masks = [mask_new]
    pvs = [lambda a: jnp.dot(a, v_new, preferred_element_type=F32)]
    for j in reversed(range(n_pages)):
        zs.append(lambda j=j: jnp.dot(qm, kp_refs[j][...].astype(BF16), preferred_element_type=F32) + bias)
        masks.append(None)
        pvs.append(lambda a, j=j: _dot_nt(a, vp_refs[j][...].astype(BF16)))
    chain = (zs, masks, pvs, jnp.zeros((nrow, W_A), F32), jnp.zeros((nrow, 1), F32))
    ((o, _),) = _sb_sweep([chain], tri, SAMPLE_PIPELINE_LAG)
    out = jnp.zeros((SUBLANES, W_A), F32)
    for h in range(H_A):
        out = jnp.where(head_of_lane == h, o[h * SUBLANES:(h + 1) * SUBLANES, :], out)
    o_ref[...] = out


def _attn_sample(q8, k8, v8, cache_kt, cache_vt, page_table, bias, l):
    bs = q8.shape[0]
    n_pages = page_table.shape[1]
    page = cache_kt.shape[3]
    new_spec = pl.BlockSpec((None, SUBLANES, W_A), lambda bi, pt, bias: (bi, 0, 0))

    def page_spec(j):
        return pl.BlockSpec((None, None, W_A, page), lambda bi, pt, bias, j=j: (pt[bi, j], l, 0, 0))

    grid_spec = pltpu.PrefetchScalarGridSpec(
        num_scalar_prefetch=2,
        grid=(bs,),
        in_specs=[new_spec, new_spec, new_spec]
        + [page_spec(j) for j in range(n_pages)] + [page_spec(j) for j in range(n_pages)],
        out_specs=new_spec,
    )
    return pl.pallas_call(
        functools.partial(_attn_sample_body, n_pages=n_pages, page=page),
        grid_spec=grid_spec,
        out_shape=jax.ShapeDtypeStruct((bs, SUBLANES, W_A), F32),
        compiler_params=_cparams("parallel"),
        name="attn_sample",
    )(page_table, bias, q8, k8, v8, *([cache_kt] * n_pages), *([cache_vt] * n_pages))


def _group_sum(x, ones_bd):
    nt = x.shape[1] // LANES
    stacked = jnp.concatenate([x[:, i * LANES:(i + 1) * LANES] for i in range(nt)], axis=0)
    s = _dot_exact_rhs(stacked, ones_bd)
    m = x.shape[0]
    return jnp.concatenate([s[i * m:(i + 1) * m] for i in range(nt)], axis=1)


def _rwkv_body(cols_ref, shift_ref, s0_ref, mu_ref, w0_ref, wwa_ref, a0_ref, g2_ref, kk_ref, ka_ref,
               rk_ref, lnw_ref, lnb_ref, ones_ref, ob_ref, sout_ref, s_ref, prev_ref, *, t_valid, padded):
    c = pl.program_id(1)
    nc = pl.num_programs(1)
    nb, ch, _ = cols_ref.shape
    m = nb * ch
    hps = HEADS_PER_SYSTEM
    grp = hps * D_B
    n_grp = W_B // grp
    sub = hps * ch

    @pl.when(c == 0)
    def _():
        prev_ref[...] = shift_ref[...]
        s_ref[...] = jnp.zeros_like(s_ref)
        for e in range(nb):
            for h in range(H_B):
                g, hh = divmod(h, hps)
                s_ref[e, g, hh * D_B:(hh + 1) * D_B, hh * D_B:(hh + 1) * D_B] = s0_ref[e, h]

    cols3 = cols_ref[...]
    cols = cols3.reshape(m, R_B)
    row = _iota2((m, 1), 0)
    first = jnp.broadcast_to(prev_ref[...], (nb, ch, R_B)).reshape(m, R_B)
    prev = jnp.where(row % ch == 0, first, pltpu.roll(cols, 1, 0))
    prev_ref[...] = cols3[:, ch - 1:ch, :]
    mixed = cols + mu_ref[...] * (prev - cols)
    r = mixed[:, 0:W_B]
    k = mixed[:, W_B:2 * W_B]
    v = mixed[:, 2 * W_B:3 * W_B]
    lora = mixed[:, 3 * W_B:3 * W_B + LORA_W + LORA_A]
    gl = mixed[:, 3 * W_B + LORA_W + LORA_A:]
    lora = jnp.where(_iota2(lora.shape, 1) < LORA_W, jnp.tanh(lora), lora)
    wa = _dot(lora, wwa_ref[...])
    w_log = -_softplus(-(w0_ref[...] + wa[:, :W_B])) - 0.5
    logw = -jnp.exp(w_log)
    a = _sigmoid(a0_ref[...] + wa[:, W_B:])
    g = _dot(_sigmoid(gl), g2_ref[...])
    ones_bd = ones_ref[...]
    kk = k * kk_ref[...]
    k2 = k * (1.0 + (a - 1.0) * ka_ref[...])
    sums = _group_sum(jnp.concatenate([kk * kk, r * k2 * rk_ref[...]], axis=0), ones_bd)
    kk = kk * lax.rsqrt(sums[0:m] + L2_EPS)
    bonus = sums[m:2 * m] * v
    v_s = v
    if padded:
        valid = (row % ch) + c * ch < t_valid
        logw = jnp.where(valid, logw, 0.0)
        kk = jnp.where(valid, kk, 0.0)
        k2 = jnp.where(valid, k2, 0.0)
        v_s = jnp.where(valid, v, 0.0)
    a_s = -kk
    b_s = kk * a

    cum = _dot_exact_lhs(_time_prefix_matrix(m, ch), logw)
    e_in = jnp.exp(cum)
    e_neg = jnp.exp(-cum)
    at = a_s * jnp.exp(cum - logw)
    bt = b_s * e_neg
    kt = k2 * e_neg
    rt = r * e_in

    nbs = SYSTEM_ROWS // sub
    n = nbs * sub
    bm = ((_iota2((n, grp), 0) // ch) % hps) == (_iota2((n, grp), 1) // D_B)
    pr = _iota2((n, n), 0)
    pc = _iota2((n, n), 1)
    same = (pr // ch) == (pc // ch)
    strict = same & (pc < pr)
    incl = same & (pc <= pr)
    n_levels = max(1, (ch - 1).bit_length())

    systems = [(range(e0, e0 + nbs), gi, slice(gi * grp, (gi + 1) * grp))
               for e0 in range(0, nb, nbs) for gi in range(n_grp)]
    per_seq = lambda x: [x[i * sub:(i + 1) * sub] for i in range(nbs)]
    cat = lambda xs: xs[0] if len(xs) == 1 else jnp.concatenate(xs, axis=0)

    def bd(x):
        out = []
        for seqs, _, sl in systems:
            pieces = [x[e * ch:(e + 1) * ch, sl] for e in seqs for _ in range(hps)]
            out.append(jnp.where(bm, jnp.concatenate(pieces, axis=0), 0.0).astype(BF16))
        return out

    ab, bb, kb, rb, vb = (bd(x) for x in (at, bt, kt, rt, v_s))
    ar = [jnp.concatenate([a_, r_], axis=0) for a_, r_ in zip(ab, rb)]
    s_old = [[s_ref[e, gi] for e in seqs] for seqs, gi, _ in systems]
    s16 = [[s.astype(BF16) for s in ss] for ss in s_old]
    ar_b = [_dot_nt(x, y_) for x, y_ in zip(ar, bb)]
    ar_k = [_dot_nt(x, y_) for x, y_ in zip(ar, kb)]
    l_ab = [jnp.where(strict, x[0:n], 0.0) for x in ar_b]
    m_rb = [jnp.where(incl, x[n:2 * n], 0.0).astype(BF16) for x in ar_b]
    l_ak = [jnp.where(strict, x[0:n], 0.0) for x in ar_k]
    m_rk = [jnp.where(incl, x[n:2 * n], 0.0) for x in ar_k]
    a_s0 = [cat([_dot_nt(x, s) for x, s in zip(per_seq(a_), ss)]) for a_, ss in zip(ab, s16)]
    r_s0 = [cat([_dot_nt(x, s) for x, s in zip(per_seq(r_), ss)]) for r_, ss in zip(rb, s16)]
    rhs = [x + _dot(l, v_) for x, l, v_ in zip(a_s0, l_ak, vb)]
    y_v = [x + _dot(mk, v_) for x, mk, v_ in zip(r_s0, m_rk, vb)]
    eye = jnp.where(pr == pc, 1.0, 0.0)
    inv = [eye + l for l in l_ab]
    pw = l_ab
    for _ in range(n_levels - 1):
        pw = [_dot(p_, p_) for p_ in pw]
        inv = [i_ + _dot(i_, p_) for i_, p_ in zip(inv, pw)]
    ub = [_dot(i_, x).astype(BF16) for i_, x in zip(inv, rhs)]
    y_parts = {}
    for si, (seqs, gi, sl) in enumerate(systems):
        y = y_v[si] + jnp.dot(m_rb[si], ub[si], preferred_element_type=F32)
        for e, ye in zip(seqs, per_seq(y)):
            y_parts[e, gi] = sum(ye[h * ch:(h + 1) * ch] for h in range(hps))
        parts = zip(seqs, per_seq(ub[si]), per_seq(bb[si]), per_seq(vb[si]), per_seq(kb[si]), s_old[si])
        for e, ue, be, ve, ke, so in parts:
            wc = e_in[e * ch + ch - 1:e * ch + ch, sl]
            s_ref[e, gi] = (so + _dot_tn(ue, be) + _dot_tn(ve, ke)) * wc
    y = cat([jnp.concatenate([y_parts[e, gi] for gi in range(n_grp)], axis=1) for e in range(nb)])

    inv_d = 1.0 / D_B
    dev = y - _group_sum(y, ones_bd) * inv_d
    var = _group_sum(dev * dev, ones_bd) * inv_d
    yn = dev * lax.rsqrt(var + LN_X_EPS) * lnw_ref[...] + lnb_ref[...]
    ob_ref[...] = ((yn + bonus) * g).reshape(nb, ch, W_B)

    @pl.when(c == nc - 1)
    def _():
        for e in range(nb):
            for h in range(H_B):
                gq, hh = divmod(h, hps)
                sout_ref[e, h] = s_ref[e, gq, hh * D_B:(hh + 1) * D_B, hh * D_B:(hh + 1) * D_B]


def _seqs_per_step(b, ch, heads, wanted_systems):
    nbs = SYSTEM_ROWS // (heads * ch)
    nb = nbs * wanted_systems
    while nb > nbs and b % nb:
        nb -= nbs
    assert b % nb == 0
    return nb


def _rwkv(cols, shift, s0, prm, l, t_valid, ch):
    b, t, _ = cols.shape
    nb = _seqs_per_step(b, ch, HEADS_PER_SYSTEM, 2 if t > ch else 1)
    nc = t // ch
    assert t % ch == 0 and (ch == PROMPT_CHUNK or nc == 1)
    vec = lambda n: pl.BlockSpec((None, 1, n), lambda bi, c: (l, 0, 0))
    mat = lambda k, n: pl.BlockSpec((None, k, n), lambda bi, c: (l, 0, 0))
    return pl.pallas_call(
        functools.partial(_rwkv_body, t_valid=t_valid, padded=t_valid < t),
        grid=(b // nb, nc),
        in_specs=[
            pl.BlockSpec((nb, ch, R_B), lambda bi, c: (bi, c, 0)),
            pl.BlockSpec((nb, 1, R_B), lambda bi, c: (bi, 0, 0)),
            pl.BlockSpec((nb, H_B, D_B, D_B), lambda bi, c: (bi, 0, 0, 0)),
            vec(R_B), vec(W_B), mat(LORA_W + LORA_A, 2 * W_B), vec(W_B), mat(LORA_G, W_B),
            vec(W_B), vec(W_B), vec(W_B), vec(W_B), vec(W_B),
            pl.BlockSpec((LANES, LANES), lambda bi, c: (0, 0)),
        ],
        out_specs=[
            pl.BlockSpec((nb, ch, W_B), lambda bi, c: (bi, c, 0)),
            pl.BlockSpec((nb, H_B, D_B, D_B), lambda bi, c: (bi, 0, 0, 0)),
        ],
        out_shape=[jax.ShapeDtypeStruct((b, t, W_B), F32),
                   jax.ShapeDtypeStruct((b, H_B, D_B, D_B), F32)],
        scratch_shapes=[pltpu.VMEM((nb, W_B // (HEADS_PER_SYSTEM * D_B), HEADS_PER_SYSTEM * D_B,
                                    HEADS_PER_SYSTEM * D_B), F32),
                        pltpu.VMEM((nb, 1, R_B), F32)],
        compiler_params=_cparams("parallel", "arbitrary"),
        name="rwkv7",
    )(cols, shift, s0, prm["mu"], prm["w0"], prm["wwa"], prm["a0"], prm["g2"], prm["k_k"], prm["k_a"],
      prm["r_k"], prm["ln_w"], prm["ln_b"], prm["ones_bd"])


def _gdn_body(x_ref, ab_ref, z_ref, cprev_ref, s0_ref, cw_ref, alog_ref, dtb_ref, nw_ref,
              oc_ref, sout_ref, s_ref, xbuf_ref, *, t_valid, padded):
    c = pl.program_id(1)
    nc = pl.num_programs(1)
    nb, ch, _ = x_ref.shape
    m = nb * ch
    hist = SUBLANES

    @pl.when(c == 0)
    def _():
        xbuf_ref[:, 0:hist, :] = cprev_ref[...]
        s_ref[...] = s0_ref[...]

    x3 = x_ref[...]
    xbuf_ref[:, hist:hist + ch, :] = x3
    cw = cw_ref[...]
    conv = x3.reshape(m, 3 * W_C) * cw[CONV_W - 1:CONV_W, :]
    for i in range(1, CONV_W):
        conv = conv + xbuf_ref[:, hist - i:hist - i + ch, :].reshape(m, 3 * W_C) * cw[CONV_W - 1 - i:CONV_W - i, :]
    conv = conv * _sigmoid(conv)
    xbuf_ref[:, 0:hist, :] = x3[:, ch - hist:ch, :]

    def stack(y):
        return jnp.concatenate([y[e * ch:(e + 1) * ch, h * D_C:(h + 1) * D_C]
                                for e in range(nb) for h in range(H_C)], axis=0)

    q = stack(conv[:, 0:W_C])
    k = stack(conv[:, W_C:2 * W_C])
    v = stack(conv[:, 2 * W_C:3 * W_C])
    q = q * lax.rsqrt(jnp.sum(q * q, axis=-1, keepdims=True) + L2_EPS) * (D_C ** -0.5)
    k = k * lax.rsqrt(jnp.sum(k * k, axis=-1, keepdims=True) + L2_EPS)

    ab = ab_ref[...].reshape(m, LANES)
    g_all = -jnp.exp(alog_ref[...]) * _softplus(ab + dtb_ref[...])
    beta_all = _sigmoid(ab)
    if padded:
        valid = (_iota2((m, 1), 0) % ch) + c * ch < t_valid
        g_all = jnp.where(valid, g_all, 0.0)
        beta_all = jnp.where(valid, beta_all, 0.0)
    cum_all = _dot_exact_lhs(_time_prefix_matrix(m, ch), g_all)

    def rows(src, lane0, last_only=False):
        out = []
        for e in range(nb):
            for h in range(H_C):
                lo = e * ch + (ch - 1 if last_only else 0)
                piece = src[lo:(e + 1) * ch, lane0 + h:lane0 + h + 1]
                out.append(jnp.broadcast_to(piece, (ch, D_C)))
        return jnp.concatenate(out, axis=0)

    cum = rows(cum_all, 0)
    beta = rows(beta_all, H_C)
    g_last = rows(cum_all, 0, last_only=True)

    n = SYSTEM_ROWS
    n_sys = nb * H_C * ch // n
    per_sys = lambda y: [y[i * n:(i + 1) * n] for i in range(n_sys)]
    pr = _iota2((n, n), 0)
    pc = _iota2((n, n), 1)
    same = (pr // ch) == (pc // ch)
    strict = same & (pc < pr)
    incl = same & (pc <= pr)
    decay = []
    for cum_s in per_sys(cum):
        cum_col = jnp.concatenate([cum_s] * (n // D_C), axis=1)
        cum_row = jnp.broadcast_to(jnp.transpose(cum_s)[0:1, :], (n, n))
        decay.append(jnp.where(incl, jnp.exp(jnp.where(incl, cum_col - cum_row, 0.0)), 0.0))

    kbeta = k * beta
    kb16 = k.astype(BF16)
    e_cum = jnp.exp(cum)
    kk_t = [_dot_nt(x, y) for x, y in zip(per_sys(kbeta), per_sys(kb16))]
    qk_t = [_dot_nt(x, y) for x, y in zip(per_sys(q), per_sys(kb16))]
    neg_a = [jnp.where(strict, -(x * d), 0.0) for x, d in zip(kk_t, decay)]
    eye = jnp.where(pr == pc, 1.0, 0.0)
    inv = [eye + l_ for l_ in neg_a]
    pw = neg_a
    for _ in range(max(1, (ch - 1).bit_length()) - 1):
        pw = [_dot(p_, p_) for p_ in pw]
        inv = [i_ + _dot(i_, p_) for i_, p_ in zip(inv, pw)]
    vk = jnp.concatenate([v * beta, kbeta * e_cum], axis=1)
    uw = jnp.concatenate([_dot(i_, x) for i_, x in zip(inv, per_sys(vk))], axis=0)
    u_cap = uw[:, :D_C]
    w_cap = uw[:, D_C:]
    qk = [x * d for x, d in zip(qk_t, decay)]
    q_dec = q * e_cum
    k_dec = k * jnp.exp(g_last - cum)

    heads = [(e, h, slice((e * H_C + h) * ch, (e * H_C + h + 1) * ch)) for e in range(nb) for h in range(H_C)]
    s_old = [s_ref[e, h] for e, h, _ in heads]
    ws = [_dot(jnp.concatenate([w_cap[sl], q_dec[sl]], axis=0), s) for (_, _, sl), s in zip(heads, s_old)]
    v_new = [u_cap[sl] - w[0:ch] for (_, _, sl), w in zip(heads, ws)]
    o_state = [w[ch:2 * ch] for w in ws]
    for (e, h, sl), s, vn in zip(heads, s_old, v_new):
        s_ref[e, h] = s * jnp.exp(g_last[sl.start:sl.start + 1, :]) + _dot_tn(k_dec[sl], vn)
    vn_all = jnp.concatenate(v_new, axis=0)
    o = jnp.concatenate(o_state, axis=0) + jnp.concatenate(
        [_dot(x, y) for x, y in zip(qk, per_sys(vn_all))], axis=0)

    z3 = z_ref[...]
    for e in range(nb):
        for h in range(H_C):
            i = e * H_C + h
            oh = o[i * ch:(i + 1) * ch]
            oh = oh * lax.rsqrt(jnp.mean(oh * oh, axis=-1, keepdims=True) + NORM_EPS) * nw_ref[...]
            zh = z3[e, :, h * D_C:(h + 1) * D_C]
            oc_ref[e, :, h * D_C:(h + 1) * D_C] = oh * (zh * _sigmoid(zh))

    @pl.when(c == nc - 1)
    def _():
        sout_ref[...] = s_ref[...]


def _gdn(x, ab, z, cprev, s0, prm, l, t_valid, ch):
    b, t, _ = x.shape
    nb = _seqs_per_step(b, ch, H_C, 4 if t > ch else 1)
    nc = t // ch
    assert t % ch == 0 and ch >= SUBLANES
    vec = lambda n: pl.BlockSpec((None, 1, n), lambda bi, c: (l, 0, 0))
    return pl.pallas_call(
        functools.partial(_gdn_body, t_valid=t_valid, padded=t_valid < t),
        grid=(b // nb, nc),
        in_specs=[
            pl.BlockSpec((nb, ch, 3 * W_C), lambda bi, c: (bi, c, 0)),
            pl.BlockSpec((nb, ch, LANES), lambda bi, c: (bi, c, 0)),
            pl.BlockSpec((nb, ch, W_C), lambda bi, c: (bi, c, 0)),
            pl.BlockSpec((nb, SUBLANES, 3 * W_C), lambda bi, c: (bi, 0, 0)),
            pl.BlockSpec((nb, H_C, D_C, D_C), lambda bi, c: (bi, 0, 0, 0)),
            pl.BlockSpec((None, CONV_W, 3 * W_C), lambda bi, c: (l, 0, 0)),
            vec(LANES), vec(LANES), vec(D_C),
        ],
        out_specs=[
            pl.BlockSpec((nb, ch, W_C), lambda bi, c: (bi, c, 0)),
            pl.BlockSpec((nb, H_C, D_C, D_C), lambda bi, c: (bi, 0, 0, 0)),
        ],
        out_shape=[jax.ShapeDtypeStruct((b, t, W_C), F32),
                   jax.ShapeDtypeStruct((b, H_C, D_C, D_C), F32)],
        scratch_shapes=[pltpu.VMEM((nb, H_C, D_C, D_C), F32),
                        pltpu.VMEM((nb, SUBLANES + ch, 3 * W_C), F32)],
        compiler_params=_cparams("parallel", "arbitrary"),
        name="gdn",
    )(x, ab, z, cprev, s0, prm["conv_w"], prm["a_log"], prm["dt_bias"], prm["norm_w"])


def _pad_time(x, t_to):
    return jnp.pad(x, ((0, 0), (0, t_to - x.shape[1]), (0, 0)))


def kernel(x_prompt, x_sample, cache_k, cache_v, page_table, state_rwkv, state_rwkv_shift, state_gdn,
           state_gdn_conv, ffn1_norm, ffn1_w_gate, ffn1_w_up, ffn1_w_down, mix_norm, w_in, sb_bias,
           rwkv_mu, rwkv_w0, rwkv_w2, rwkv_a0, rwkv_a2, rwkv_g2, rwkv_k_k, rwkv_k_a, rwkv_r_k,
           rwkv_ln_w, rwkv_ln_b, gdn_conv_w, gdn_a_log, gdn_dt_bias, gdn_norm_w,
           w_br_a, w_br_b, w_br_c, w_out, ffn2_norm, ffn2_w_gate, ffn2_w_up, ffn2_w_down, final_norm):
    bp, t, d = x_prompt.shape
    bs, ts, _ = x_sample.shape
    n_layers = ffn1_norm.shape[0]
    n_pool, _, page, _, _ = cache_k.shape
    assert t % PROMPT_CHUNK == 0 and ts <= SUBLANES

    bf = lambda w: w.astype(BF16)
    row3 = lambda w: w.reshape(n_layers, 1, -1)
    f1g, f1u, f1d = bf(ffn1_w_gate), bf(ffn1_w_up), bf(ffn1_w_down)
    f2g, f2u, f2d = bf(ffn2_w_gate), bf(ffn2_w_up), bf(ffn2_w_down)
    o_b = 3 * W_A
    o_c = o_b + R_B
    o_ab = o_c + 3 * W_C
    o_z = o_ab + 2 * H_C
    w1 = bf(w_in[:, :, :o_c])
    w2 = bf(jnp.concatenate([
        w_in[:, :, o_c:o_ab],
        jnp.pad(w_in[:, :, o_ab:o_z], ((0, 0), (0, 0), (0, LANES - 2 * H_C))),
        w_in[:, :, o_z:], ], axis=-1))
    widths1 = (W_A, W_A, W_A, R_B)
    widths2 = (3 * W_C, LANES, W_C, 3 * d)
    wba, wbb, wbc, wo = bf(w_br_a), bf(w_br_b), bf(w_br_c), bf(w_out)
    zeros_wa = jnp.zeros((n_layers, LORA_W, W_B), F32)
    wwa = bf(jnp.concatenate([
        jnp.concatenate([rwkv_w2, zeros_wa], axis=-1),
        jnp.concatenate([zeros_wa, rwkv_a2], axis=-1)], axis=1))
    gi = jnp.arange(LANES) // D_B
    rwkv_prm = dict(
        mu=row3(rwkv_mu), w0=row3(rwkv_w0), wwa=wwa, a0=row3(rwkv_a0), g2=bf(rwkv_g2),
        k_k=row3(rwkv_k_k), k_a=row3(rwkv_k_a), r_k=row3(rwkv_r_k), ln_w=row3(rwkv_ln_w),
        ln_b=row3(rwkv_ln_b), ones_bd=(gi[:, None] == gi[None, :]).astype(BF16))
    lane_pad = lambda w: jnp.pad(w, ((0, 0), (0, LANES - w.shape[1]))).reshape(n_layers, 1, LANES)
    gdn_prm = dict(conv_w=gdn_conv_w, a_log=lane_pad(gdn_a_log), dt_bias=lane_pad(gdn_dt_bias),
                   norm_w=row3(gdn_norm_w))
    n1a, n1b, n2 = row3(ffn1_norm), row3(mix_norm), row3(ffn2_norm)
    cache_kt = jnp.transpose(cache_k, (0, 1, 3, 4, 2)).reshape(n_pool, n_layers, W_A, page)
    cache_vt = jnp.transpose(cache_v, (0, 1, 3, 4, 2)).reshape(n_pool, n_layers, W_A, page)

    xp = x_prompt.reshape(bp * t, d)
    xs = x_sample.reshape(bs * ts, d)
    zero_shift = jnp.zeros((bp, 1, R_B), F32)
    zero_rwkv = jnp.zeros((bp, H_B, D_B, D_B), F32)
    zero_conv = jnp.zeros((bp, SUBLANES, 3 * W_C), F32)
    zero_gdn = jnp.zeros((bp, H_C, D_C, D_C), F32)
    pad8 = lambda y: _pad_time(y.reshape(bs, ts, -1), SUBLANES)

    outs = [[] for _ in range(10)]
    kv_stacks = None
    for l in range(n_layers):
        xp = _ffn(xp, n1a, f1g, f1u, f1d, l)
        xs = _ffn(xs, n1a, f1g, f1u, f1d, l)
        qp, kpt, vpt, bcp = _proj(xp, n1b, w1, l, widths1, TM_DENSE, transposed=(1, 2), stacks=kv_stacks,
                                  batch_time=(bp, t))
        kv_stacks = (kpt, vpt)
        cxp, cabp, czp, gtp = _proj(xp, n1b, w2, l, widths2, TM_WIDE_PROJ, sigmoid_last=True)
        qs, ks, vs, bcs = _proj(xs, n1b, w1, l, widths1, TM_DENSE)
        cxs, cabs, czs, gts = _proj(xs, n1b, w2, l, widths2, TM_WIDE_PROJ, sigmoid_last=True)

        oap = _attn_prompt(qp, kpt, vpt, sb_bias[l], l, bp, t)
        oas = _attn_sample(pad8(qs), pad8(ks), pad8(vs), cache_kt, cache_vt, page_table, sb_bias[l], l)
        oas = oas[:, :ts].reshape(bs * ts, W_A)

        bcp3 = bcp.reshape(bp, t, R_B)
        obp, rwp = _rwkv(bcp3, zero_shift, zero_rwkv, rwkv_prm, l, t, PROMPT_CHUNK)
        bcs3 = bcs.reshape(bs, ts, R_B)
        obs, rws = _rwkv(pad8(bcs), state_rwkv_shift[l][:, None, :], state_rwkv[l], rwkv_prm, l, ts, SUBLANES)
        obs = obs[:, :ts].reshape(bs * ts, W_B)

        cxp3 = cxp.reshape(bp, t, 3 * W_C)
        ocp, gdp = _gdn(cxp3, cabp.reshape(bp, t, LANES), czp.reshape(bp, t, W_C), zero_conv, zero_gdn,
                        gdn_prm, l, t, PROMPT_CHUNK)
        cxs3 = cxs.reshape(bs, ts, 3 * W_C)
        conv_full = jnp.concatenate([state_gdn_conv[l], cxs3], axis=1)
        cprev_s = jnp.pad(state_gdn_conv[l], ((0, 0), (SUBLANES - (CONV_W - 1), 0), (0, 0)))
        ocs, gds = _gdn(pad8(cxs), pad8(cabs), pad8(czs), cprev_s, state_gdn[l], gdn_prm, l, ts, SUBLANES)
        ocs = ocs[:, :ts].reshape(bs * ts, W_C)

        xp = _merge(xp, oap, obp.reshape(bp * t, W_B), ocp.reshape(bp * t, W_C), gtp, wba, wbb, wbc, wo, l)
        xs = _merge(xs, oas, obs, ocs, gts, wba, wbb, wbc, wo, l)
        xp = _ffn(xp, n2, f2g, f2u, f2d, l)
        xs = _ffn(xs, n2, f2g, f2u, f2d, l)

        per_layer = (ks.reshape(bs, ts, H_A, D_A), vs.reshape(bs, ts, H_A, D_A),
                     rwp, rws, bcp3[:, t - 1], bcs3[:, ts - 1], gdp, gds,
                     cxp3[:, t - (CONV_W - 1):], conv_full[:, ts:])
        for acc, val in zip(outs, per_layer):
            acc.append(val)

    y_prompt = _final_norm(xp, final_norm.reshape(1, d)).reshape(bp, t, d)
    y_sample = _final_norm(xs, final_norm.reshape(1, d)).reshape(bs, ts, d)
    k_prompt, v_prompt = (jnp.transpose(s.reshape(bp, n_layers, H_A, D_A, t), (0, 1, 4, 2, 3)) for s in kv_stacks)
    stack_axes = (1, 1, 0, 0, 0, 0, 0, 0, 0, 0)
    rest = tuple(jnp.stack(v, axis=ax) for v, ax in zip(outs, stack_axes))
    return (y_prompt, y_sample, k_prompt, v_prompt) + rest
```
